```python
import math
import jax
import jax.numpy as jnp
from jax import lax
import numpy as np

D_MODEL = 2048
BATCH = 16
SEQ = 2048
DEPTH = 2
DEC_BATCH = 2
DEC_SEQ = 16384
PAST_LEN = 128

N_MIXERS = 2
N_A_LAYERS = (DEPTH + 1) // 2
N_B_LAYERS = DEPTH // 2
RWKV_HEAD = 64
RWKV_HEADS = D_MODEL // RWKV_HEAD
DECAY_LORA = 96
AAA_LORA = 96
GATE_LORA = 256
GN_EPS = 64e-5
ATT_HEAD_DIM = 128
ATT_HEADS = D_MODEL // ATT_HEAD_DIM
DILATED_GROUPS = ((128, 1), (512, 4), (2048, 16))
N_GROUPS = len(DILATED_GROUPS)
ATT_SCALE = 1.0 / math.sqrt(ATT_HEAD_DIM)
NEG_INF = -1e30
NUM_BUCKETS = 32
MAX_DISTANCE = 1024
FFN_HIDDEN = 5632
PLE_DIM = 256
NORM_EPS = 1e-6

kernel_name = "hybrid_rwkv7_dilated_attn_encoder"


def _rmsnorm(x, g):
    xf = x.astype(jnp.float32)
    y = xf * lax.rsqrt(jnp.mean(xf * xf, axis=-1, keepdims=True) + NORM_EPS)
    return (y * g.astype(jnp.float32)).astype(x.dtype)


def _wkv_scan(r, w, k, v, kk, a, reverse):
    B, T, H, N = r.shape
    xs = tuple(jnp.moveaxis(t.astype(jnp.float32), 1, 0) for t in (r, w, k, v, kk, a))

    def step(S, inp):
        r_t, w_t, k_t, v_t, kk_t, a_t = inp
        sa = jnp.einsum('bhvk,bhk->bhv', S, -kk_t)
        S = (S * w_t[:, :, None, :]
             + sa[..., None] * (kk_t * a_t)[:, :, None, :]
             + v_t[..., None] * k_t[:, :, None, :])
        return S, jnp.einsum('bhvk,bhk->bhv', S, r_t)

    S0 = jnp.zeros((B, H, N, N), jnp.float32)
    _, ys = lax.scan(step, S0, xs, reverse=reverse)
    return jnp.moveaxis(ys, 0, 1)


def _rwkv7_bidir(x, mu, w_rkv, w0, w1, w2, a0, a1, a2, g1, g2, k_k, k_a, r_k, lnx_g, lnx_b, w_o):
    B, T, D = x.shape
    H, N = RWKV_HEADS, RWKV_HEAD
    zero = jnp.zeros_like(x[:, :1])
    prev = jnp.concatenate([zero, x[:, :-1]], axis=1)
    nxt = jnp.concatenate([x[:, 1:], zero], axis=1)
    xx = 0.5 * (prev + nxt) - x
    xs = x[None] + xx[None] * mu[:, None, None, :]
    rkv = jnp.einsum('jbtd,jde->jbte', xs[:3], w_rkv)
    r, k, v = rkv[0], rkv[1], rkv[2]
    xw, xa, xg = xs[3], xs[4], xs[5]
    g = jax.nn.sigmoid(xg @ g1) @ g2

    def heads(t):
        return t.reshape(B, T, H, N)

    kk = heads(k * k_k).astype(jnp.float32)
    kk = kk * lax.rsqrt(jnp.sum(kk * kk, axis=-1, keepdims=True) + 1e-12)
    y = 0.0
    k_bonus = 0.0
    for j in range(2):
        w_log = -jax.nn.softplus(-(w0[j] + jnp.tanh(xw @ w1[j]) @ w2[j]).astype(jnp.float32)) - 0.5
        decay = jnp.exp(-jnp.exp(w_log))
        a = jax.nn.sigmoid(a0[j] + (xa @ a1[j]) @ a2[j])
        kj = k * (1.0 + (a - 1.0) * k_a)
        y = y + _wkv_scan(heads(r), heads(decay), heads(kj), heads(v), kk, heads(a), reverse=(j == 1))
        k_bonus = k_bonus + 0.5 * kj
    mean = jnp.mean(y, axis=-1, keepdims=True)
    var = jnp.mean(jnp.square(y - mean), axis=-1, keepdims=True)
    yn = ((y - mean) * lax.rsqrt(var + GN_EPS)).reshape(B, T, D) * lnx_g + lnx_b
    bonus = jnp.sum(heads(r) * heads(k_bonus) * r_k, axis=-1, keepdims=True) * heads(v)
    out = ((yn + bonus.reshape(B, T, D)) * g).astype(x.dtype)
    return out @ w_o


def _t5_bucket(rel):
    half = NUM_BUCKETS // 2
    max_exact = half // 2
    ret = jnp.where(rel > 0, half, 0)
    n = jnp.abs(rel)
    nf = jnp.maximum(n, 1).astype(jnp.float32)
    large = max_exact + (jnp.log(nf / max_exact) / math.log(MAX_DISTANCE / max_exact)
                         * (half - max_exact)).astype(jnp.int32)
    large = jnp.minimum(large, half - 1)
    return ret + jnp.where(n < max_exact, n, large)


def _rel_bias(table_g, dil, side):
    i = jnp.arange(side)[:, None]
    j = jnp.arange(3 * side)[None, :]
    rel = dil * (j - side - i)
    return jnp.transpose(table_g[_t5_bucket(rel)], (2, 0, 1))


def _dilated_group(q, k, v, bias, dil, side):
    B, T, H, dh = q.shape
    L = T // dil
    nb = -(-L // side)
    Lp = nb * side

    def cls(t):
        return t.reshape(B, L, dil, H, dh).transpose(0, 2, 1, 3, 4)

    qc = jnp.pad(cls(q), ((0, 0), (0, 0), (0, Lp - L), (0, 0), (0, 0))).reshape(B, dil, nb, side, H, dh)

    def band(t):
        tp = jnp.pad(cls(t), ((0, 0), (0, 0), (side, Lp - L + side), (0, 0), (0, 0)))
        tp = tp.reshape(B, dil, nb + 2, side, H, dh)
        return jnp.concatenate([tp[:, :, :-2], tp[:, :, 1:-1], tp[:, :, 2:]], axis=3)

    kb, vb = band(k), band(v)
    i = jnp.arange(side)[:, None]
    j = jnp.arange(3 * side)[None, :]
    off = j - side - i
    kpos = jnp.arange(nb)[:, None, None] * side + j[None] - side
    mask = (jnp.abs(off)[None] <= side) & (kpos >= 0) & (kpos < L)
    logits = jnp.einsum('bcnqhd,bcnkhd->bcnhqk', qc, kb).astype(jnp.float32) * ATT_SCALE
    logits = logits + bias[None, None, None].astype(jnp.float32)
    logits = jnp.where(mask[None, None, :, None], logits, NEG_INF)
    m = jnp.max(logits, axis=-1, keepdims=True)
    pexp = jnp.exp(logits - m)
    s = jnp.sum(pexp, axis=-1, keepdims=True)
    o = jnp.einsum('bcnhqk,bcnkhd->bcnqhd', (pexp / s).astype(v.dtype), vb)
    lse = (m + jnp.log(s))[..., 0]
    o = o.reshape(B, dil, Lp, H, dh)[:, :, :L].transpose(0, 2, 1, 3, 4).reshape(B, T, H, dh)
    lse = lse.transpose(0, 1, 2, 4, 3).reshape(B, dil, Lp, H)[:, :, :L].transpose(0, 2, 1, 3).reshape(B, T, H)
    return o, lse


def _dilated_attention(x, w_qkv, w_o, rel_table):
    B, T, D = x.shape
    qkv = (x @ w_qkv).reshape(B, T, N_GROUPS, 3, ATT_HEADS, ATT_HEAD_DIM)
    outs, lses = [], []
    for gi, (win, dil) in enumerate(DILATED_GROUPS):
        side = win // (2 * dil)
        bias = _rel_bias(rel_table[:, gi * ATT_HEADS:(gi + 1) * ATT_HEADS], dil, side)
        o, lse = _dilated_group(qkv[:, :, gi, 0], qkv[:, :, gi, 1], qkv[:, :, gi, 2], bias, dil, side)
        outs.append(o)
        lses.append(lse)
    wts = jax.nn.softmax(jnp.stack(lses), axis=0)
    o = jnp.sum(wts[..., None] * jnp.stack(outs).astype(jnp.float32), axis=0).astype(x.dtype)
    return o.reshape(B, T, D) @ w_o


def _swiglu(x, w_gu, w_down):
    h = x @ w_gu
    gate, up = h[..., :FFN_HIDDEN], h[..., FFN_HIDDEN:]
    return (jax.nn.silu(gate) * up) @ w_down


def _trunk(x, p, norm_gains, rel_bias_table, rwkv_mu, rwkv_w_rkv, rwkv_w0, rwkv_w1, rwkv_w2,
           rwkv_a0, rwkv_a1, rwkv_a2, rwkv_g1, rwkv_g2, rwkv_k_k, rwkv_k_a, rwkv_r_k,
           rwkv_lnx_g, rwkv_lnx_b, rwkv_w_o, att_w_qkv, att_w_o, ffn_w_gu, ffn_w_down,
           ple_norm, ple_w_gate, ple_w_proj):
    for i in range(DEPTH):
        g = norm_gains[i]
        h = _rmsnorm(x, g[0])
        j = i // N_MIXERS
        if i % N_MIXERS == 0:
            h = _rwkv7_bidir(h, rwkv_mu[j], rwkv_w_rkv[j], rwkv_w0[j], rwkv_w1[j], rwkv_w2[j],
                             rwkv_a0[j], rwkv_a1[j], rwkv_a2[j], rwkv_g1[j], rwkv_g2[j],
                             rwkv_k_k[j], rwkv_k_a[j], rwkv_r_k[j], rwkv_lnx_g[j], rwkv_lnx_b[j], rwkv_w_o[j])
        else:
            h = _dilated_attention(h, att_w_qkv[j], att_w_o[j], rel_bias_table)
        x = x + _rmsnorm(h, g[1])
        h = _swiglu(_rmsnorm(x, g[2]), ffn_w_gu[i], ffn_w_down[i])
        x = x + _rmsnorm(h, g[3])
        gate = jax.nn.sigmoid(_rmsnorm(x, ple_norm[i]) @ ple_w_gate[i])
        x = x + gate * (p[i] @ ple_w_proj[i])
    return x


def setup_inputs(seed: int = 0) -> dict:
    key = jax.random.key(seed)
    ks = iter(jax.random.split(key, 40))
    D, F = D_MODEL, FFN_HIDDEN

    def nrm(shape, scale):
        return jax.random.normal(next(ks), shape, jnp.float32) * scale

    def uni(shape, lo, hi):
        return jax.random.uniform(next(ks), shape, jnp.float32, minval=lo, maxval=hi)

    NA, NB = N_A_LAYERS, N_B_LAYERS
    return {
        "x_prompt": nrm((BATCH, SEQ, D), 1.0),
        "x_sample": nrm((DEC_BATCH, DEC_SEQ, D), 1.0),
        "p_prompt": nrm((DEPTH, BATCH, SEQ, PLE_DIM), 1.0),
        "p_sample": nrm((DEPTH, DEC_BATCH, DEC_SEQ, PLE_DIM), 1.0),
        "norm_gains": 1.0 + nrm((DEPTH, 4, D), 0.1),
        "rel_bias_table": nrm((NUM_BUCKETS, N_GROUPS * ATT_HEADS), 0.5),
        "rwkv_mu": uni((NA, 6, D), 0.0, 1.0),
        "rwkv_w_rkv": nrm((NA, 3, D, D), D ** -0.5),
        "rwkv_w0": uni((NA, 2, D), -3.0, 1.0),
        "rwkv_w1": nrm((NA, 2, D, DECAY_LORA), D ** -0.5),
        "rwkv_w2": nrm((NA, 2, DECAY_LORA, D), 0.5 * DECAY_LORA ** -0.5),
        "rwkv_a0": nrm((NA, 2, D), 0.5),
        "rwkv_a1": nrm((NA, 2, D, AAA_LORA), D ** -0.5),
        "rwkv_a2": nrm((NA, 2, AAA_LORA, D), 0.5 * AAA_LORA ** -0.5),
        "rwkv_g1": nrm((NA, D, GATE_LORA), D ** -0.5),
        "rwkv_g2": nrm((NA, GATE_LORA, D), GATE_LORA ** -0.5),
        "rwkv_k_k": 0.85 + nrm((NA, D), 0.1),
        "rwkv_k_a": 1.0 + nrm((NA, D), 0.1),
        "rwkv_r_k": nrm((NA, RWKV_HEADS, RWKV_HEAD), 0.1),
        "rwkv_lnx_g": 1.0 + nrm((NA, D), 0.1),
        "rwkv_lnx_b": nrm((NA, D), 0.02),
        "rwkv_w_o": nrm((NA, D, D), D ** -0.5),
        "att_w_qkv": nrm((NB, D, N_GROUPS * 3 * ATT_HEADS * ATT_HEAD_DIM), D ** -0.5),
        "att_w_o": nrm((NB, ATT_HEADS * ATT_HEAD_DIM, D), D ** -0.5),
        "ffn_w_gu": nrm((DEPTH, D, 2 * F), D ** -0.5),
        "ffn_w_down": nrm((DEPTH, F, D), F ** -0.5),
        "ple_norm": 1.0 + nrm((DEPTH, D), 0.1),
        "ple_w_gate": nrm((DEPTH, D, D), D ** -0.5),
        "ple_w_proj": nrm((DEPTH, PLE_DIM, D), PLE_DIM ** -0.5),
    }


def reference(x_prompt, x_sample, p_prompt, p_sample, norm_gains, rel_bias_table, rwkv_mu, rwkv_w_rkv,
              rwkv_w0, rwkv_w1, rwkv_w2, rwkv_a0, rwkv_a1, rwkv_a2, rwkv_g1, rwkv_g2, rwkv_k_k, rwkv_k_a,
              rwkv_r_k, rwkv_lnx_g, rwkv_lnx_b, rwkv_w_o, att_w_qkv, att_w_o, ffn_w_gu, ffn_w_down,
              ple_norm, ple_w_gate, ple_w_proj):
    weights = (norm_gains, rel_bias_table, rwkv_mu, rwkv_w_rkv, rwkv_w0, rwkv_w1, rwkv_w2,
               rwkv_a0, rwkv_a1, rwkv_a2, rwkv_g1, rwkv_g2, rwkv_k_k, rwkv_k_a, rwkv_r_k,
               rwkv_lnx_g, rwkv_lnx_b, rwkv_w_o, att_w_qkv, att_w_o, ffn_w_gu, ffn_w_down,
               ple_norm, ple_w_gate, ple_w_proj)
    y_prompt = _trunk(x_prompt, p_prompt, *weights)
    y_sample = _trunk(x_sample, p_sample, *weights)
    return (y_prompt, y_sample)
```

```python
import functools
import math

import numpy as np
import jax
import jax.numpy as jnp
from jax import lax
from jax.experimental import pallas as pl
from jax.experimental.pallas import tpu as pltpu

F32 = jnp.float32
BF16 = jnp.bfloat16

NORM_EPS = 1e-6
GN_EPS = 64e-5
RWKV_HEAD = 64
ATT_HEAD_DIM = 128
DILATED_GROUPS = ((128, 1), (512, 4), (2048, 16))
ATT_SCALE = 1.0 / math.sqrt(ATT_HEAD_DIM)
NEG_INF = -1e30
NUM_BUCKETS = 32
MAX_DISTANCE = 1024

LANES = 128
V7X_VMEM_BYTES = 64 * 1024 * 1024
VMEM_CAP = V7X_VMEM_BYTES - 8 * 1024 * 1024

WKV_CHUNK = 64
ATT_SIDE = 64


def _vmem_limit(est_bytes):
    return int(min(max(2 * est_bytes, 32 * 1024 * 1024), VMEM_CAP))


def _nbytes(shape, dtype):
    return int(np.prod(shape)) * jnp.dtype(dtype).itemsize


_DN = {"nn": (((1,), (0,)), ((), ())), "nt": (((1,), (1,)), ((), ())), "tn": (((0,), (0,)), ((), ()))}


def _dot(a, b, dims="nn", f32=False):
    if f32:
        return lax.dot_general(a.astype(F32), b.astype(F32), _DN[dims],
                               precision=lax.Precision.HIGHEST, preferred_element_type=F32)
    return lax.dot_general(a.astype(BF16), b.astype(BF16), _DN[dims], preferred_element_type=F32)


def _rms(x, g):
    return x * lax.rsqrt(jnp.mean(x * x, axis=-1, keepdims=True) + NORM_EPS) * g


def _fused_mm_kernel(*refs, nd, nr, nv, no, nk, epilogue):
    d = refs[:2 * nd]
    r = refs[2 * nd:2 * nd + nr]
    v = refs[2 * nd + nr:2 * nd + nr + nv]
    o = refs[2 * nd + nr + nv:2 * nd + nr + nv + no]
    scratch = refs[2 * nd + nr + nv + no:]

    def finish(accs):
        outs = epilogue(accs, [x[...] for x in r], [x[...] for x in v])
        for o_ref, val in zip(o, outs):
            o_ref[...] = val.astype(o_ref.dtype)

    if nk == 1:
        finish([jnp.dot(d[2 * t][...], d[2 * t + 1][...], preferred_element_type=F32) for t in range(nd)])
    else:
        acc_ref = scratch[0]
        k = pl.program_id(2)

        @pl.when(k == 0)
        def _():
            acc_ref[...] = jnp.zeros_like(acc_ref)

        acc_ref[...] += jnp.dot(d[0][...], d[1][...], preferred_element_type=F32)

        @pl.when(k == nk - 1)
        def _():
            finish([acc_ref[...]])


def _fused_mm(dots, rows, vecs, epilogue, out_dtypes, n_cols, *, tm, tn, tk=None, name):
    M = dots[0][0].shape[0]
    tm = min(tm, M)
    tn = max(t for t in range(LANES, min(tn, n_cols) + 1, LANES) if n_cols % t == 0)
    assert M % tm == 0 and n_cols % tn == 0
    nd, nr, nv, no = len(dots), len(rows), len(vecs), len(out_dtypes)
    K0 = dots[0][0].shape[1]
    nk = 1
    if tk is not None and K0 > tk:
        assert nd == 1 and K0 % tk == 0
        nk = K0 // tk
    grid = (M // tm, n_cols // tn) + ((nk,) if nk > 1 else ())
    in_specs = []
    est = 0
    for x, w, off in dots:
        K = x.shape[1]
        kb = tk if nk > 1 else K
        if nk > 1:
            in_specs.append(pl.BlockSpec((tm, kb), lambda i, j, k: (i, k)))
            in_specs.append(pl.BlockSpec((kb, tn), lambda i, j, k, off=off: (k, j + off)))
        else:
            in_specs.append(pl.BlockSpec((tm, kb), lambda i, j: (i, 0)))
            in_specs.append(pl.BlockSpec((kb, tn), lambda i, j, off=off: (0, j + off)))
        est += 2 * (_nbytes((tm, kb), x.dtype) + _nbytes((kb, tn), w.dtype)) + 2 * _nbytes((tm, tn), F32)
    for a in rows:
        in_specs.append(pl.BlockSpec((tm, tn), lambda i, j, *k: (i, j)))
        est += 2 * _nbytes((tm, tn), a.dtype)
    for a in vecs:
        in_specs.append(pl.BlockSpec((1, tn), lambda i, j, *k: (0, j)))
    out_specs = [pl.BlockSpec((tm, tn), lambda i, j, *k: (i, j)) for _ in out_dtypes]
    est += sum(2 * _nbytes((tm, tn), dt) for dt in out_dtypes)
    scratch = [pltpu.VMEM((tm, tn), F32)] if nk > 1 else []
    args = []
    for x, w, _ in dots:
        args += [x, w]
    args += list(rows) + list(vecs)
    outs = pl.pallas_call(
        functools.partial(_fused_mm_kernel, nd=nd, nr=nr, nv=nv, no=no, nk=nk, epilogue=epilogue),
        grid=grid,
        in_specs=in_specs,
        out_specs=out_specs,
        out_shape=[jax.ShapeDtypeStruct((M, n_cols), dt) for dt in out_dtypes],
        scratch_shapes=scratch,
        compiler_params=pltpu.CompilerParams(
            dimension_semantics=("arbitrary",) * len(grid), vmem_limit_bytes=_vmem_limit(est)),
        name=name,
    )(*args)
    return outs


def _seq_pos(r0, Mp, Lp, Ls):
    in_prompt = r0 < Mp
    L = jnp.where(in_prompt, Lp, Ls)
    pos = lax.rem(r0 - jnp.where(in_prompt, 0, Mp), L)
    return pos, L


def _shift_mix_kernel(x_ref, xp_ref, xn_ref, g_ref, mu_ref, *outs, tm, Mp, Lp, Ls):
    r0 = pl.program_id(0) * tm
    pos, L = _seq_pos(r0, Mp, Lp, Ls)
    g = g_ref[...]
    h = _rms(x_ref[...], g)
    hp = _rms(xp_ref[...], g)[7:8]
    hn = _rms(xn_ref[...], g)[0:1]
    hp = jnp.where(pos == 0, 0.0, hp)
    hn = jnp.where(pos + tm == L, 0.0, hn)
    row = lax.broadcasted_iota(jnp.int32, (tm, 1), 0)
    prev = jnp.where(row == 0, hp, pltpu.roll(h, 1, 0))
    nxt = jnp.where(row == tm - 1, hn, pltpu.roll(h, tm - 1, 0))
    xx = 0.5 * (prev + nxt) - h
    for j, o_ref in enumerate(outs):
        o_ref[...] = (h + xx * mu_ref[j:j + 1, :]).astype(o_ref.dtype)


def _shift_mix(x, g, mu, *, Mp, Lp, Ls, tm=256):
    M, D = x.shape
    tm = min(tm, Lp, Ls)
    nb8 = M // 8
    est = 2 * _nbytes((tm, D), F32) + 12 * _nbytes((tm, D), BF16) + 6 * _nbytes((tm, D), F32)
    return pl.pallas_call(
        functools.partial(_shift_mix_kernel, tm=tm, Mp=Mp, Lp=Lp, Ls=Ls),
        grid=(M // tm,),
        in_specs=[
            pl.BlockSpec((tm, D), lambda i: (i, 0)),
            pl.BlockSpec((8, D), lambda i: (jnp.maximum(i * (tm // 8) - 1, 0), 0)),
            pl.BlockSpec((8, D), lambda i: (jnp.minimum((i + 1) * (tm // 8), nb8 - 1), 0)),
            pl.BlockSpec((1, D), lambda i: (0, 0)),
            pl.BlockSpec((6, D), lambda i: (0, 0)),
        ],
        out_specs=[pl.BlockSpec((tm, D), lambda i: (i, 0)) for _ in range(6)],
        out_shape=[jax.ShapeDtypeStruct((M, D), BF16) for _ in range(6)],
        compiler_params=pltpu.CompilerParams(
            dimension_semantics=("arbitrary",), vmem_limit_bytes=_vmem_limit(est)),
        name="shift_mix",
    )(x, x, x, g, mu)


def _wkv_masks():
    C = WKV_CHUNK
    P = 2 * C
    t = np.arange(P)[:, None]
    s = np.arange(P)[None, :]
    bd = (t // C) == (s // C)
    tw = np.arange(C)[:, None]
    sw = np.arange(P)[None, :] % C
    sq, wide, tri = [], [], []
    for rev in (False, True):
        before = (s < t) if not rev else (s > t)
        ms = [bd, bd & ((t // 8) == (s // 8)) & before]
        for b in (8, 16, 32):
            same = (t // (2 * b)) == (s // (2 * b))
            hi_t, hi_s = (t // b) % 2, (s // b) % 2
            ms.append(same & (hi_t == (0 if rev else 1)) & (hi_s == (1 if rev else 0)))
        sq.append(np.stack(ms))
        strict = (sw < tw) if not rev else (sw > tw)
        incl = (sw <= tw) if not rev else (sw >= tw)
        wide.append(np.stack([strict, incl]))
        tc, sc = np.arange(C)[:, None], np.arange(C)[None, :]
        tri.append((sc <= tc) if not rev else (sc >= tc))
    return (np.stack(sq).astype(np.float32), np.stack(wide).astype(np.float32),
            np.stack(tri).astype(np.float32))


def _wkv_pair(r, k, v, lw, a, kk, k_a, S, sqm, wdm, tri):
    bd = sqm[0] > 0
    G = _dot(tri, lw, f32=True)
    gam = jnp.exp(G)
    gam_prev = jnp.exp(G - lw)
    inv_gam = jnp.exp(-G)
    g_end = jnp.sum(lw, axis=0, keepdims=True)
    to_end = jnp.exp(g_end - G)
    kj = k * (1.0 + (a - 1.0) * k_a)
    al = -kk * gam_prev
    rb = r * gam
    beta = kk * a
    bt = beta * inv_gam
    kt = kj * inv_gam

    def stack_bd(x):
        return jnp.where(bd, jnp.concatenate([x, x], axis=0), 0.0)

    lhs = jnp.concatenate([al, rb], axis=0)
    rhs = jnp.concatenate([stack_bd(bt), stack_bd(kt)], axis=0)
    x1 = _dot(lhs, rhs, "nt")
    C = r.shape[0]
    strict, incl = wdm[0] > 0, wdm[1] > 0
    a_ab = jnp.where(strict, x1[:C, :LANES], 0.0)
    a_ak = jnp.where(strict, x1[:C, LANES:], 0.0)
    m_rb = jnp.where(incl, x1[C:, :LANES], 0.0)
    m_rk = jnp.where(incl, x1[C:, LANES:], 0.0)

    aa = jnp.concatenate([a_ab, a_ab], axis=0)
    a8 = aa * sqm[1]
    eye = (lax.broadcasted_iota(jnp.int32, (2 * C, 2 * C), 0)
           == lax.broadcasted_iota(jnp.int32, (2 * C, 2 * C), 1)).astype(F32)
    tinv = eye + a8
    a2 = _dot(a8, a8, f32=True)
    tinv = tinv + _dot(tinv, a2, f32=True)
    a4 = _dot(a2, a2, f32=True)
    tinv = tinv + _dot(tinv, a4, f32=True)
    for lvl in (2, 3, 4):
        off = aa * sqm[lvl]
        tinv = tinv + _dot(tinv, _dot(off, tinv))
    tw = tinv[:C] + tinv[C:]

    v_st = stack_bd(v)
    x = _dot(al, S, "nt") + _dot(a_ak, v_st)
    u = _dot(tw, stack_bd(x))
    y = _dot(rb, S, "nt") + _dot(m_rb, stack_bd(u)) + _dot(m_rk, v_st)
    uv = jnp.concatenate([u, v], axis=0)
    bk = jnp.concatenate([beta * to_end, kj * to_end], axis=0)
    s_new = jnp.where(bd, S * jnp.exp(g_end) + _dot(uv, bk, "tn"), 0.0)
    return y, s_new


def _wkv_kernel(rf, kf, vf, lwf, af, rb_, kb_, vb_, lwb, ab, kk_ref, ka_ref, sqm_ref, wdm_ref, tri_ref,
                yf_ref, yb_ref, s_ref, *, npairs, nc, Mp, Lp, Ls):
    c = pl.program_id(1)
    C = WKV_CHUNK
    pos_f, _ = _seq_pos(c * C, Mp, Lp, Ls)
    pos_b, L_b = _seq_pos((nc - 1 - c) * C, Mp, Lp, Ls)

    @pl.when(pos_f == 0)
    def _():
        s_ref[0] = jnp.zeros_like(s_ref[0])

    @pl.when(pos_b + C == L_b)
    def _():
        s_ref[1] = jnp.zeros_like(s_ref[1])

    ones_bd = sqm_ref[0, 0]
    for d, (r_ref, k_ref, v_ref, lw_ref, a_ref, y_ref) in enumerate(
            ((rf, kf, vf, lwf, af, yf_ref), (rb_, kb_, vb_, lwb, ab, yb_ref))):
        for p in range(npairs):
            sl = slice(p * LANES, (p + 1) * LANES)
            k = k_ref[:, sl]
            kr = k * kk_ref[:, sl]
            ss = _dot(kr * kr, ones_bd, f32=True)
            kk = kr * lax.rsqrt(ss + 1e-12)
            y, s_new = _wkv_pair(r_ref[:, sl], k, v_ref[:, sl], lw_ref[:, sl], a_ref[:, sl], kk,
                                 ka_ref[:, sl], s_ref[d, p], sqm_ref[d], wdm_ref[d], tri_ref[d])
            y_ref[:, sl] = y
            s_ref[d, p] = s_new


def _wkv(r, k, v, lw, a, k_k, k_a, *, Mp, Lp, Ls, lane_block=256):
    M, D = r.shape
    C = WKV_CHUNK
    lb = min(lane_block, D)
    nlb, nc = D // lb, M // C
    sqm, wdm, tri = (jnp.asarray(m) for m in _wkv_masks())
    fwd = lambda j, c: (c, j)
    bwd = lambda j, c: (nc - 1 - c, j)
    bwd2 = lambda j, c: (nc - 1 - c, nlb + j)
    blk = lambda im: pl.BlockSpec((C, lb), im)
    const = lambda shape: pl.BlockSpec(shape, lambda j, c: (0,) * len(shape))
    est = 24 * _nbytes((C, lb), F32) + 4 * _nbytes(sqm.shape, F32)
    return pl.pallas_call(
        functools.partial(_wkv_kernel, npairs=lb // LANES, nc=nc, Mp=Mp, Lp=Lp, Ls=Ls),
        grid=(nlb, nc),
        in_specs=[blk(fwd), blk(fwd), blk(fwd), blk(fwd), blk(fwd),
                  blk(bwd), blk(bwd), blk(bwd), blk(bwd2), blk(bwd2),
                  pl.BlockSpec((1, lb), lambda j, c: (0, j)), pl.BlockSpec((1, lb), lambda j, c: (0, j)),
                  const(sqm.shape), const(wdm.shape), const(tri.shape)],
        out_specs=[blk(fwd), blk(bwd)],
        out_shape=[jax.ShapeDtypeStruct((M, D), F32)] * 2,
        scratch_shapes=[pltpu.VMEM((2, lb // LANES, LANES, LANES), F32)],
        compiler_params=pltpu.CompilerParams(
            dimension_semantics=("arbitrary", "arbitrary"), vmem_limit_bytes=_vmem_limit(est)),
        name="wkv7_bidir",
    )(r, k, v, lw, a, r, k, v, lw, a, k_k, k_a, sqm, wdm, tri)


def _wkv_post_kernel(yf_ref, yb_ref, r_ref, k_ref, v_ref, a0_ref, a1_ref, g_ref,
                     ka_ref, rk_ref, lg_ref, lb_ref, ones_ref, o_ref):
    ones_bd = ones_ref[...]
    D = o_ref.shape[1]
    inv_n = 1.0 / RWKV_HEAD
    for p in range(D // LANES):
        sl = slice(p * LANES, (p + 1) * LANES)
        y = yf_ref[:, sl] + yb_ref[:, sl]
        mean = _dot(y, ones_bd, f32=True) * inv_n
        dlt = y - mean
        var = _dot(dlt * dlt, ones_bd, f32=True) * inv_n
        yn = dlt * lax.rsqrt(var + GN_EPS) * lg_ref[:, sl] + lb_ref[:, sl]
        k, ka = k_ref[:, sl], ka_ref[:, sl]
        k0 = k * (1.0 + (a0_ref[:, sl] - 1.0) * ka)
        k1 = k * (1.0 + (a1_ref[:, sl] - 1.0) * ka)
        kb = 0.5 * k0 + 0.5 * k1
        bonus = _dot(r_ref[:, sl] * kb * rk_ref[:, sl], ones_bd, f32=True) * v_ref[:, sl]
        o_ref[:, sl] = ((yn + bonus) * g_ref[:, sl]).astype(o_ref.dtype)


def _wkv_post(yf, yb, r, k, v, a, g, k_a, r_k, lnx_g, lnx_b, *, tm=256):
    M, D = r.shape
    tm = min(tm, M)
    row = pl.BlockSpec((tm, D), lambda i: (i, 0))
    row2 = pl.BlockSpec((tm, D), lambda i: (i, 1))
    vec = pl.BlockSpec((1, D), lambda i: (0, 0))
    P = 2 * RWKV_HEAD
    ones_bd = jnp.asarray((np.arange(P)[:, None] // RWKV_HEAD == np.arange(P)[None, :] // RWKV_HEAD)
                          .astype(np.float32))
    est = 2 * 8 * _nbytes((tm, D), F32) + 2 * _nbytes((tm, D), BF16)
    return pl.pallas_call(
        _wkv_post_kernel,
        grid=(M // tm,),
        in_specs=[row, row, row, row, row, row, row2, row, vec, vec, vec, vec,
                  pl.BlockSpec((P, P), lambda i: (0, 0))],
        out_specs=row,
        out_shape=jax.ShapeDtypeStruct((M, D), BF16),
        compiler_params=pltpu.CompilerParams(
            dimension_semantics=("arbitrary",), vmem_limit_bytes=_vmem_limit(est)),
        name="wkv_post",
    )(yf, yb, r, k, v, a, a, g, k_a, r_k, lnx_g, lnx_b, ones_bd)


def _attn_kernel(q_ref, kp_ref, kc_ref, kn_ref, vp_ref, vc_ref, vn_ref, bias_ref, o_ref, lse_ref,
                 *, heads, nb_prompt_total, nb_p, nb_s):
    n = pl.program_id(0)
    side = ATT_SIDE
    in_prompt = n < nb_prompt_total
    nb = jnp.where(in_prompt, nb_p, nb_s)
    pos = lax.rem(n - jnp.where(in_prompt, 0, nb_prompt_total), nb)
    has_prev = pos > 0
    has_next = pos < nb - 1
    i = lax.broadcasted_iota(jnp.int32, (side, 3 * side), 0)
    j = lax.broadcasted_iota(jnp.int32, (side, 3 * side), 1)
    off = j - side - i
    valid = (jnp.abs(off) <= side) & ((j >= side) | has_prev) & ((j < 2 * side) | has_next)
    lane = lax.broadcasted_iota(jnp.int32, (side, LANES), 1)
    lse_all = jnp.zeros((side, LANES), F32)
    for h in range(heads):
        sl = slice(h * ATT_HEAD_DIM, (h + 1) * ATT_HEAD_DIM)
        kcat = jnp.concatenate([kp_ref[:, sl], kc_ref[:, sl], kn_ref[:, sl]], axis=0)
        vcat = jnp.concatenate([vp_ref[:, sl], vc_ref[:, sl], vn_ref[:, sl]], axis=0)
        s = _dot(q_ref[:, sl], kcat, "nt") * ATT_SCALE + bias_ref[h]
        s = jnp.where(valid, s, NEG_INF)
        m = jnp.max(s, axis=-1, keepdims=True)
        p = jnp.exp(s - m)
        l = jnp.sum(p, axis=-1, keepdims=True)
        o_ref[:, sl] = _dot(p / l, vcat).astype(o_ref.dtype)
        lse_all = jnp.where(lane == h, m + jnp.log(l), lse_all)
    lse_ref[...] = lse_all


def _dilated_group(qkv, bias, gi, dil, *, Mp, Lp, Ls, heads):
    M, W = qkv.shape
    HD = heads * ATT_HEAD_DIM
    side = ATT_SIDE
    rows = M // dil
    q2 = qkv.reshape(rows, dil * W)
    nbt = rows // side
    per_class = W // HD
    nb_p, nb_s = Lp // dil // side, Ls // dil // side
    col = lambda t: (lambda n, c: c * per_class + gi * 3 + t)
    cur = lambda t: pl.BlockSpec((side, HD), lambda n, c, t=t: (n, c * per_class + gi * 3 + t))
    prv = lambda t: pl.BlockSpec((side, HD), lambda n, c, t=t: (jnp.maximum(n - 1, 0), c * per_class + gi * 3 + t))
    nxt = lambda t: pl.BlockSpec((side, HD), lambda n, c, t=t: (jnp.minimum(n + 1, nbt - 1), c * per_class + gi * 3 + t))
    del col
    est = 2 * 8 * _nbytes((side, HD), BF16) + 2 * _nbytes(bias.shape, F32)
    o, lse = pl.pallas_call(
        functools.partial(_attn_kernel, heads=heads, nb_prompt_total=Mp // dil // side, nb_p=nb_p, nb_s=nb_s),
        grid=(nbt, dil),
        in_specs=[cur(0), prv(1), cur(1), nxt(1), prv(2), cur(2), nxt(2),
                  pl.BlockSpec(bias.shape, lambda n, c: (0, 0, 0))],
        out_specs=[pl.BlockSpec((side, HD), lambda n, c: (n, c)),
                   pl.BlockSpec((side, LANES), lambda n, c: (n, c))],
        out_shape=[jax.ShapeDtypeStruct((rows, dil * HD), BF16),
                   jax.ShapeDtypeStruct((rows, dil * LANES), F32)],
        compiler_params=pltpu.CompilerParams(
            dimension_semantics=("arbitrary", "arbitrary"), vmem_limit_bytes=_vmem_limit(est)),
        name=f"dilated_attn_g{gi}",
    )(q2, q2, q2, q2, q2, q2, q2, bias)
    return o.reshape(M, HD), lse.reshape(M, LANES)


def _combine_kernel(o0, o1, o2, l0, l1, l2, e_ref, out_ref):
    ls = [l0[...], l1[...], l2[...]]
    m = jnp.maximum(jnp.maximum(ls[0], ls[1]), ls[2])
    es = [jnp.exp(x - m) for x in ls]
    tot = es[0] + es[1] + es[2]
    acc = None
    for o_ref, e in zip((o0, o1, o2), es):
        w = _dot(e / tot, e_ref[...], f32=True)
        term = w * o_ref[...].astype(F32)
        acc = term if acc is None else acc + term
    out_ref[...] = acc.astype(out_ref.dtype)


def _combine_groups(os_, lses, *, heads, tm=256):
    M, HD = os_[0].shape
    tm = min(tm, M)
    expand = np.zeros((LANES, HD), np.float32)
    for h in range(heads):
        expand[h, h * ATT_HEAD_DIM:(h + 1) * ATT_HEAD_DIM] = 1.0
    row = pl.BlockSpec((tm, HD), lambda i: (i, 0))
    lrow = pl.BlockSpec((tm, LANES), lambda i: (i, 0))
    est = 2 * 4 * _nbytes((tm, HD), BF16) + 6 * _nbytes((tm, HD), F32)
    return pl.pallas_call(
        _combine_kernel,
        grid=(M // tm,),
        in_specs=[row, row, row, lrow, lrow, lrow, pl.BlockSpec((LANES, HD), lambda i: (0, 0))],
        out_specs=row,
        out_shape=jax.ShapeDtypeStruct((M, HD), BF16),
        compiler_params=pltpu.CompilerParams(
            dimension_semantics=("arbitrary",), vmem_limit_bytes=_vmem_limit(est)),
        name="attn_combine",
    )(*os_, *lses, jnp.asarray(expand))


def _t5_bucket(rel):
    half = NUM_BUCKETS // 2
    max_exact = half // 2
    ret = jnp.where(rel > 0, half, 0)
    n = jnp.abs(rel)
    nf = jnp.maximum(n, 1).astype(F32)
    large = max_exact + (jnp.log(nf / max_exact) / math.log(MAX_DISTANCE / max_exact)
                         * (half - max_exact)).astype(jnp.int32)
    large = jnp.minimum(large, half - 1)
    return ret + jnp.where(n < max_exact, n, large)


def _rel_bias(table_g, dil, side):
    i = jnp.arange(side)[:, None]
    j = jnp.arange(3 * side)[None, :]
    rel = dil * (j - side - i)
    return jnp.transpose(table_g[_t5_bucket(rel)], (2, 0, 1)).astype(F32)


def _ep_cast(accs, rows, vecs):
    return tuple(accs)


def _ep_lora1(accs, rows, vecs):
    return jnp.tanh(accs[0]), accs[1], jax.nn.sigmoid(accs[2])


def _ep_lora2(accs, rows, vecs):
    lw = -math.exp(-0.5) * jax.nn.sigmoid(vecs[0] + accs[0])
    return lw, jax.nn.sigmoid(vecs[1] + accs[1])


def _ep_residual_norm(accs, rows, vecs):
    x_new = rows[0] + _rms(accs[0], vecs[0])
    return x_new, _rms(x_new, vecs[1])


def _ep_swiglu(accs, rows, vecs):
    return (jax.nn.silu(accs[0]) * accs[1],)


def _ep_ple(accs, rows, vecs):
    return (rows[0] + jax.nn.sigmoid(accs[0]) * accs[1],)


def _ep_ple_norm(accs, rows, vecs):
    x_new = rows[0] + jax.nn.sigmoid(accs[0]) * accs[1]
    return x_new, _rms(x_new, vecs[0])


def _pad_cols(w, n):
    return jnp.pad(w, ((0, 0), (0, n - w.shape[1])))


def _block_rows(w0, w1, n):
    h = n // 2
    z0 = jnp.zeros_like(w0)
    top = jnp.pad(jnp.concatenate([w0, z0], axis=1), ((0, h - w0.shape[0]), (0, 0)))
    bot = jnp.pad(jnp.concatenate([z0, w1], axis=1), ((0, h - w1.shape[0]), (0, 0)))
    return jnp.concatenate([top, bot], axis=0)


def _rwkv_layer(x, g_in, g_out, g_next, p, *, Mp, Lp, Ls):
    M, D = x.shape
    (mu, w_rkv, w0, w1, w2, a0, a1, a2, g1, g2, k_k, k_a, r_k, lnx_g, lnx_b, w_o) = p
    xs = _shift_mix(x, g_in[None], mu, Mp=Mp, Lp=Lp, Ls=Ls)
    wr, wk, wv = (w_rkv[i].astype(BF16) for i in range(3))
    r, = _fused_mm([(xs[0], wr, 0)], [], [], _ep_cast, [F32], D, tm=512, tn=1024, name="proj_r")
    k, = _fused_mm([(xs[1], wk, 0)], [], [], _ep_cast, [F32], D, tm=512, tn=1024, name="proj_k")
    v, = _fused_mm([(xs[2], wv, 0)], [], [], _ep_cast, [F32], D, tm=512, tn=1024, name="proj_v")

    LP = 2 * LANES
    half = LP // 2
    w1c = jnp.concatenate([_pad_cols(w1[0], half), _pad_cols(w1[1], half)], axis=1).astype(BF16)
    a1c = jnp.concatenate([_pad_cols(a1[0], half), _pad_cols(a1[1], half)], axis=1).astype(BF16)
    g1c = _pad_cols(g1, LP).astype(BF16)
    tw, ta, sg = _fused_mm([(xs[3], w1c, 0), (xs[4], a1c, 0), (xs[5], g1c, 0)], [], [], _ep_lora1,
                           [BF16, BF16, BF16], LP, tm=512, tn=LP, name="lora_in")
    w2c = _block_rows(w2[0], w2[1], LP).astype(BF16)
    a2c = _block_rows(a2[0], a2[1], LP).astype(BF16)
    w0c = jnp.concatenate([w0[0], w0[1]])[None]
    a0c = jnp.concatenate([a0[0], a0[1]])[None]
    lw, a = _fused_mm([(tw, w2c, 0), (ta, a2c, 0)], [], [w0c, a0c], _ep_lora2, [F32, F32], 2 * D,
                      tm=512, tn=1024, name="lora_out_wa")
    g2c = jnp.pad(g2, ((0, LP - g2.shape[0]), (0, 0))).astype(BF16)
    g, = _fused_mm([(sg, g2c, 0)], [], [], _ep_cast, [F32], D, tm=512, tn=1024, name="lora_out_g")

    yf, yb = _wkv(r, k, v, lw, a, k_k[None], k_a[None], Mp=Mp, Lp=Lp, Ls=Ls)
    o = _wkv_post(yf, yb, r, k, v, a, g, k_a[None], r_k.reshape(1, D), lnx_g[None], lnx_b[None])
    return _fused_mm([(o, w_o.astype(BF16), 0)], [x], [g_out[None], g_next[None]], _ep_residual_norm,
                     [F32, BF16], D, tm=256, tn=D, name="rwkv_out")


def _attn_layer(x, xn, g_out, g_next, w_qkv, w_o, rel_table, *, Mp, Lp, Ls):
    M, D = x.shape
    heads = D // ATT_HEAD_DIM
    W = w_qkv.shape[1]
    qkv, = _fused_mm([(xn, w_qkv.astype(BF16), 0)], [], [], _ep_cast, [BF16], W, tm=512, tn=1024, name="qkv")
    os_, lses = [], []
    for gi, (win, dil) in enumerate(DILATED_GROUPS):
        side = win // (2 * dil)
        assert side == ATT_SIDE and (Lp // dil) % side == 0 and (Ls // dil) % side == 0
        bias = _rel_bias(rel_table[:, gi * heads:(gi + 1) * heads], dil, side)
        o, lse = _dilated_group(qkv, bias, gi, dil, Mp=Mp, Lp=Lp, Ls=Ls, heads=heads)
        os_.append(o)
        lses.append(lse)
    o = _combine_groups(os_, lses, heads=heads)
    return _fused_mm([(o, w_o.astype(BF16), 0)], [x], [g_out[None], g_next[None]], _ep_residual_norm,
                     [F32, BF16], D, tm=256, tn=D, name="attn_out")


def _ffn_ple(x, xn, g_out, ple_norm, w_gu, w_down, p, w_gate, w_proj, g_next):
    M, D = x.shape
    F = w_down.shape[0]
    tn = 512
    assert F % tn == 0
    wgu = w_gu.astype(BF16)
    hmid, = _fused_mm([(xn, wgu, 0), (xn, wgu, F // tn)], [], [], _ep_swiglu, [BF16], F,
                      tm=512, tn=tn, name="ffn_up")
    x, xn = _fused_mm([(hmid, w_down.astype(BF16), 0)], [x], [g_out[None], ple_norm[None]],
                      _ep_residual_norm, [F32, BF16], D, tm=256, tn=D, tk=512, name="ffn_down")
    dots = [(xn, w_gate.astype(BF16), 0), (p, w_proj.astype(BF16), 0)]
    if g_next is None:
        x, = _fused_mm(dots, [x], [], _ep_ple, [F32], D, tm=256, tn=D, name="ple_last")
        return x, None
    return _fused_mm(dots, [x], [g_next[None]], _ep_ple_norm, [F32, BF16], D, tm=256, tn=D, name="ple")


def _first_norm_kernel(x_ref, g_ref, o_ref):
    o_ref[...] = _rms(x_ref[...], g_ref[...]).astype(o_ref.dtype)


def _first_norm(x, g, *, tm=512):
    M, D = x.shape
    tm = min(tm, M)
    return pl.pallas_call(
        _first_norm_kernel,
        grid=(M // tm,),
        in_specs=[pl.BlockSpec((tm, D), lambda i: (i, 0)), pl.BlockSpec((1, D), lambda i: (0, 0))],
        out_specs=pl.BlockSpec((tm, D), lambda i: (i, 0)),
        out_shape=jax.ShapeDtypeStruct((M, D), BF16),
        compiler_params=pltpu.CompilerParams(dimension_semantics=("arbitrary",)),
        name="first_norm",
    )(x, g[None])


def kernel(x_prompt, x_sample, p_prompt, p_sample, norm_gains, rel_bias_table, rwkv_mu, rwkv_w_rkv, rwkv_w0, rwkv_w1, rwkv_w2, rwkv_a0, rwkv_a1, rwkv_a2, rwkv_g1, rwkv_g2, rwkv_k_k, rwkv_k_a, rwkv_r_k, rwkv_lnx_g, rwkv_lnx_b, rwkv_w_o, att_w_qkv, att_w_o, ffn_w_gu, ffn_w_down, ple_norm, ple_w_gate, ple_w_proj):
    Bp, Lp, D = x_prompt.shape
    Bs, Ls, _ = x_sample.shape
    Mp, Ms = Bp * Lp, Bs * Ls
    depth = norm_gains.shape[0]
    x = jnp.concatenate([x_prompt.reshape(Mp, D), x_sample.reshape(Ms, D)], axis=0)
    pdim = p_prompt.shape[-1]
    p = jnp.concatenate([p_prompt.reshape(depth, Mp, pdim), p_sample.reshape(depth, Ms, pdim)],
                        axis=1).astype(BF16)
    seq = dict(Mp=Mp, Lp=Lp, Ls=Ls)
    xn = None
    for i in range(depth):
        g = norm_gains[i]
        j = i // 2
        if i % 2 == 0:
            params = (rwkv_mu[j], rwkv_w_rkv[j], rwkv_w0[j], rwkv_w1[j], rwkv_w2[j], rwkv_a0[j], rwkv_a1[j],
                      rwkv_a2[j], rwkv_g1[j], rwkv_g2[j], rwkv_k_k[j], rwkv_k_a[j], rwkv_r_k[j],
                      rwkv_lnx_g[j], rwkv_lnx_b[j], rwkv_w_o[j])
            x, xn = _rwkv_layer(x, g[0], g[1], g[2], params, **seq)
        else:
            if xn is None:
                xn = _first_norm(x, g[0])
            x, xn = _attn_layer(x, xn, g[1], g[2], att_w_qkv[j], att_w_o[j], rel_bias_table, **seq)
        g_next = norm_gains[i + 1][0] if (i + 1 < depth and (i + 1) % 2 == 1) else None
        x, xn = _ffn_ple(x, xn, g[3], ple_norm[i], ffn_w_gu[i], ffn_w_down[i], p[i],
                         ple_w_gate[i], ple_w_proj[i], g_next)
    return x[:Mp].reshape(Bp, Lp, D), x[Mp:].reshape(Bs, Ls, D)
```

```python
import functools
import math

import numpy as np
import jax
import jax.numpy as jnp
from jax import lax
from jax.experimental import pallas as pl
from jax.experimental.pallas import tpu as pltpu

F32 = jnp.float32
BF16 = jnp.bfloat16

NORM_EPS = 1e-6
GN_EPS = 64e-5
RWKV_HEAD = 64
ATT_HEAD_DIM = 128
DILATED_GROUPS = ((128, 1), (512, 4), (2048, 16))
ATT_SCALE = 1.0 / math.sqrt(ATT_HEAD_DIM)
NEG_INF = -1e30
NUM_BUCKETS = 32
MAX_DISTANCE = 1024

LANES = 128
V7X_VMEM_BYTES = 64 * 1024 * 1024
VMEM_CAP = V7X_VMEM_BYTES - 8 * 1024 * 1024

WKV_CHUNK = 64
ATT_SIDE = 64
O_DT = jnp.bfloat16


def _vmem_limit(est_bytes):
    return int(min(max(2 * est_bytes, 32 * 1024 * 1024), VMEM_CAP))


def _nbytes(shape, dtype):
    return int(np.prod(shape)) * jnp.dtype(dtype).itemsize


_DN = {"nn": (((1,), (0,)), ((), ())), "nt": (((1,), (1,)), ((), ())), "tn": (((0,), (0,)), ((), ()))}


def _dot(a, b, dims="nn", f32=False):
    if f32:
        return lax.dot_general(a.astype(F32), b.astype(F32), _DN[dims],
                               precision=lax.Precision.HIGHEST, preferred_element_type=F32)
    return lax.dot_general(a.astype(BF16), b.astype(BF16), _DN[dims], preferred_element_type=F32)


def _segsum(x, sel):
    hi = x.astype(BF16)
    lo = (x - hi.astype(F32)).astype(BF16)
    return _dot(hi, sel) + _dot(lo, sel)


def _rms(x, g):
    return x * lax.rsqrt(jnp.mean(x * x, axis=-1, keepdims=True) + NORM_EPS) * g


def _fused_mm_kernel(*refs, nd, nr, nv, no, nk, epilogue):
    d = refs[:2 * nd]
    r = refs[2 * nd:2 * nd + nr]
    v = refs[2 * nd + nr:2 * nd + nr + nv]
    o = refs[2 * nd + nr + nv:2 * nd + nr + nv + no]
    scratch = refs[2 * nd + nr + nv + no:]

    def finish(accs):
        outs = epilogue(accs, [x[...] for x in r], [x[...] for x in v])
        for o_ref, val in zip(o, outs):
            o_ref[...] = val.astype(o_ref.dtype)

    if nk == 1:
        finish([jnp.dot(d[2 * t][...], d[2 * t + 1][...], preferred_element_type=F32) for t in range(nd)])
    else:
        acc_ref = scratch[0]
        k = pl.program_id(2)

        @pl.when(k == 0)
        def _():
            acc_ref[...] = jnp.zeros_like(acc_ref)

        acc_ref[...] += jnp.dot(d[0][...], d[1][...], preferred_element_type=F32)

        @pl.when(k == nk - 1)
        def _():
            finish([acc_ref[...]])


def _fused_mm(dots, rows, vecs, epilogue, out_dtypes, n_cols, *, tm, tn, tk=None, name):
    M = dots[0][0].shape[0]
    tm = min(tm, M)
    tn = max(t for t in range(LANES, min(tn, n_cols) + 1, LANES)
             if n_cols % t == 0 and all(c0 % t == 0 for _, _, c0 in dots))
    assert M % tm == 0 and n_cols % tn == 0
    dots = [(x, w, c0 // tn) for x, w, c0 in dots]
    nd, nr, nv, no = len(dots), len(rows), len(vecs), len(out_dtypes)
    K0 = dots[0][0].shape[1]
    nk = 1
    if tk is not None and K0 > tk:
        assert nd == 1 and K0 % tk == 0
        nk = K0 // tk
    grid = (M // tm, n_cols // tn) + ((nk,) if nk > 1 else ())
    in_specs = []
    est = 0
    for x, w, off in dots:
        K = x.shape[1]
        kb = tk if nk > 1 else K
        if nk > 1:
            in_specs.append(pl.BlockSpec((tm, kb), lambda i, j, k: (i, k)))
            in_specs.append(pl.BlockSpec((kb, tn), lambda i, j, k, off=off: (k, j + off)))
        else:
            in_specs.append(pl.BlockSpec((tm, kb), lambda i, j: (i, 0)))
            in_specs.append(pl.BlockSpec((kb, tn), lambda i, j, off=off: (0, j + off)))
        est += 2 * (_nbytes((tm, kb), x.dtype) + _nbytes((kb, tn), w.dtype)) + 2 * _nbytes((tm, tn), F32)
    for a in rows:
        in_specs.append(pl.BlockSpec((tm, tn), lambda i, j, *k: (i, j)))
        est += 2 * _nbytes((tm, tn), a.dtype)
    for a in vecs:
        in_specs.append(pl.BlockSpec((1, tn), lambda i, j, *k: (0, j)))
    out_specs = [pl.BlockSpec((tm, tn), lambda i, j, *k: (i, j)) for _ in out_dtypes]
    est += sum(2 * _nbytes((tm, tn), dt) for dt in out_dtypes)
    scratch = [pltpu.VMEM((tm, tn), F32)] if nk > 1 else []
    args = []
    for x, w, _ in dots:
        args += [x, w]
    args += list(rows) + list(vecs)
    outs = pl.pallas_call(
        functools.partial(_fused_mm_kernel, nd=nd, nr=nr, nv=nv, no=no, nk=nk, epilogue=epilogue),
        grid=grid,
        in_specs=in_specs,
        out_specs=out_specs,
        out_shape=[jax.ShapeDtypeStruct((M, n_cols), dt) for dt in out_dtypes],
        scratch_shapes=scratch,
        compiler_params=pltpu.CompilerParams(
            dimension_semantics=("arbitrary",) * len(grid), vmem_limit_bytes=_vmem_limit(est)),
        name=name,
    )(*args)
    return outs


def _seq_pos(r0, Mp, Lp, Ls):
    in_prompt = r0 < Mp
    L = jnp.where(in_prompt, Lp, Ls)
    pos = lax.rem(r0 - jnp.where(in_prompt, 0, Mp), L)
    return pos, L


def _shift_mix_kernel(x_ref, xp_ref, xn_ref, g_ref, mu_ref, *outs, tm, Mp, Lp, Ls):
    r0 = pl.program_id(0) * tm
    pos, L = _seq_pos(r0, Mp, Lp, Ls)
    g = g_ref[...]
    h = _rms(x_ref[...], g)
    hp = _rms(xp_ref[...], g)[7:8]
    hn = _rms(xn_ref[...], g)[0:1]
    hp = jnp.where(pos == 0, 0.0, hp)
    hn = jnp.where(pos + tm == L, 0.0, hn)
    row = lax.broadcasted_iota(jnp.int32, (tm, 1), 0)
    prev = jnp.where(row == 0, hp, pltpu.roll(h, 1, 0))
    nxt = jnp.where(row == tm - 1, hn, pltpu.roll(h, tm - 1, 0))
    xx = 0.5 * (prev + nxt) - h
    for j, o_ref in enumerate(outs):
        o_ref[...] = (h + xx * mu_ref[j:j + 1, :]).astype(o_ref.dtype)


def _shift_mix(x, g, mu, *, Mp, Lp, Ls, tm=256):
    M, D = x.shape
    tm = min(tm, Lp, Ls)
    nb8 = M // 8
    est = 2 * _nbytes((tm, D), F32) + 12 * _nbytes((tm, D), BF16) + 6 * _nbytes((tm, D), F32)
    return pl.pallas_call(
        functools.partial(_shift_mix_kernel, tm=tm, Mp=Mp, Lp=Lp, Ls=Ls),
        grid=(M // tm,),
        in_specs=[
            pl.BlockSpec((tm, D), lambda i: (i, 0)),
            pl.BlockSpec((8, D), lambda i: (jnp.maximum(i * (tm // 8) - 1, 0), 0)),
            pl.BlockSpec((8, D), lambda i: (jnp.minimum((i + 1) * (tm // 8), nb8 - 1), 0)),
            pl.BlockSpec((1, D), lambda i: (0, 0)),
            pl.BlockSpec((6, D), lambda i: (0, 0)),
        ],
        out_specs=[pl.BlockSpec((tm, D), lambda i: (i, 0)) for _ in range(6)],
        out_shape=[jax.ShapeDtypeStruct((M, D), BF16) for _ in range(6)],
        compiler_params=pltpu.CompilerParams(
            dimension_semantics=("arbitrary",), vmem_limit_bytes=_vmem_limit(est)),
        name="shift_mix",
    )(x, x, x, g, mu)


def _wkv_masks():
    C = WKV_CHUNK
    P = 2 * C
    t = np.arange(P)[:, None]
    s = np.arange(P)[None, :]
    bd = (t // C) == (s // C)
    tw = np.arange(C)[:, None]
    sw = np.arange(P)[None, :] % C
    sq, wide, tri = [], [], []
    for rev in (False, True):
        before = (s < t) if not rev else (s > t)
        ms = [bd, bd & ((t // 8) == (s // 8)) & before]
        for b in (8, 16, 32):
            same = (t // (2 * b)) == (s // (2 * b))
            hi_t, hi_s = (t // b) % 2, (s // b) % 2
            ms.append(same & (hi_t == (0 if rev else 1)) & (hi_s == (1 if rev else 0)))
        sq.append(np.stack(ms))
        strict = (sw < tw) if not rev else (sw > tw)
        incl = (sw <= tw) if not rev else (sw >= tw)
        wide.append(np.stack([strict, incl]))
        tc, sc = np.arange(C)[:, None], np.arange(C)[None, :]
        tri.append((sc <= tc) if not rev else (sc >= tc))
    return (np.stack(sq).astype(np.float32), np.stack(wide).astype(np.float32),
            np.stack(tri).astype(np.float32))


def _split3(x):
    hi = x.astype(BF16)
    r1 = x - hi.astype(F32)
    mid = r1.astype(BF16)
    lo = (r1 - mid.astype(F32)).astype(BF16)
    return hi, mid, lo


def _dot3(a, b):
    a_hi = a.astype(BF16)
    a_lo = (a - a_hi.astype(F32)).astype(BF16)
    b_hi = b.astype(BF16)
    b_lo = (b - b_hi.astype(F32)).astype(BF16)
    return _dot(a_hi, b_hi) + (_dot(a_hi, b_lo) + _dot(a_lo, b_hi))


def _wkv_kernel(rf, kf, vf, lwf, af, rb_, kb_, vb_, lwb, ab, kk_ref, ka_ref, sqm_ref, wdm_ref, tri_ref,
                yf_ref, yb_ref, s_ref, *, npairs, nc, Mp, Lp, Ls):
    c = pl.program_id(1)
    C = WKV_CHUNK
    pos_f, _ = _seq_pos(c * C, Mp, Lp, Ls)
    pos_b, L_b = _seq_pos((nc - 1 - c) * C, Mp, Lp, Ls)

    @pl.when(pos_f == 0)
    def _():
        s_ref[0] = jnp.zeros_like(s_ref[0])

    @pl.when(pos_b + C == L_b)
    def _():
        s_ref[1] = jnp.zeros_like(s_ref[1])

    ones_bd = sqm_ref[0, 0].astype(BF16)
    bd = sqm_ref[0, 0] > 0
    eye = (lax.broadcasted_iota(jnp.int32, (2 * C, 2 * C), 0)
           == lax.broadcasted_iota(jnp.int32, (2 * C, 2 * C), 1)).astype(F32)
    dirs = ((rf, kf, vf, lwf, af, yf_ref), (rb_, kb_, vb_, lwb, ab, yb_ref))
    units = [(d, p) for d in range(2) for p in range(npairs)]
    sls = [slice(p * LANES, (p + 1) * LANES) for p in range(npairs)]

    def stack_bd(x):
        return jnp.where(bd, jnp.concatenate([x, x], axis=0), 0.0)

    G_dir = []
    for d in range(2):
        lw = dirs[d][3][...]
        hi, mid, lo = _split3(lw)
        tri = tri_ref[d].astype(BF16)
        G_dir.append(_dot(tri, hi) + (_dot(tri, mid) + _dot(tri, lo)))

    kr = [dirs[d][1][:, sls[p]] * kk_ref[:, sls[p]] for d, p in units]
    ss = _segsum(jnp.concatenate([x * x for x in kr], axis=0), ones_bd)

    st = []
    for i, (d, p) in enumerate(units):
        r_ref, k_ref, v_ref, lw_ref, a_ref, _ = dirs[d]
        sl = sls[p]
        kk = kr[i] * lax.rsqrt(ss[i * C:(i + 1) * C] + 1e-12)
        lw, a, k = lw_ref[:, sl], a_ref[:, sl], k_ref[:, sl]
        G = G_dir[d][:, sl]
        g_end = jnp.sum(lw, axis=0, keepdims=True)
        to_end = jnp.exp(g_end - G)
        inv_gam = jnp.exp(-G)
        kj = k * (1.0 + (a - 1.0) * ka_ref[:, sl])
        beta = kk * a
        u = dict(al=-kk * jnp.exp(G - lw), rb=r_ref[:, sl] * jnp.exp(G), v_st=stack_bd(v_ref[:, sl]),
                 v=v_ref[:, sl], decay=jnp.exp(g_end),
                 bk=jnp.concatenate([beta * to_end, kj * to_end], axis=0))
        lhs = jnp.concatenate([u["al"], u["rb"]], axis=0)
        rhs = jnp.concatenate([stack_bd(beta * inv_gam), stack_bd(kj * inv_gam)], axis=0)
        u["x1"] = (lhs, rhs)
        st.append(u)

    for u in st:
        u["x1"] = _dot(u["x1"][0], u["x1"][1], "nt")
    for u, (d, p) in zip(st, units):
        strict, incl = wdm_ref[d, 0] > 0, wdm_ref[d, 1] > 0
        x1 = u.pop("x1")
        a_ab = jnp.where(strict, x1[:C, :LANES], 0.0)
        u["a_ak"] = jnp.where(strict, x1[:C, LANES:], 0.0)
        u["m_rb"] = jnp.where(incl, x1[C:, :LANES], 0.0)
        u["m_rk"] = jnp.where(incl, x1[C:, LANES:], 0.0)
        u["aa"] = jnp.concatenate([a_ab, a_ab], axis=0)
        u["a8"] = u["aa"] * sqm_ref[d, 1]
    for u in st:
        u["a2"] = _dot3(u["a8"], u["a8"])
    for u in st:
        t1 = eye + u.pop("a8")
        u["tinv"] = t1 + _dot3(t1, u["a2"])
        u["a4"] = _dot3(u["a2"], u.pop("a2"))
    for u in st:
        u["tinv"] = u["tinv"] + _dot3(u["tinv"], u.pop("a4"))
    for lvl in (2, 3, 4):
        for u, (d, p) in zip(st, units):
            u["m"] = _dot(u["aa"] * sqm_ref[d, lvl], u["tinv"])
        for u in st:
            u["tinv"] = u["tinv"] + _dot(u["tinv"], u.pop("m"))
    for u in st:
        u["xv"] = _dot(u["a_ak"], u["v_st"])
        u["yv"] = _dot(u["m_rk"], u["v_st"])
        tinv = u.pop("tinv")
        u["tw"] = tinv[:C] + tinv[C:]

    S = [s_ref[d, p] for d, p in units]
    xs = [_dot(u["al"], s, "nt") + u["xv"] for u, s in zip(st, S)]
    ys = [_dot(u["rb"], s, "nt") + u["yv"] for u, s in zip(st, S)]
    us = [_dot(u["tw"], stack_bd(x)) for u, x in zip(st, xs)]
    for i, (d, p) in enumerate(units):
        u = st[i]
        dirs[d][5][:, sls[p]] = ys[i] + _dot(u["m_rb"], stack_bd(us[i]))
        uv = jnp.concatenate([us[i], u["v"]], axis=0)
        s_ref[d, p] = jnp.where(bd, S[i] * u["decay"] + _dot(uv, u["bk"], "tn"), 0.0)


def _wkv(r, k, v, lw, a, k_k, k_a, *, Mp, Lp, Ls, lane_block=512):
    M, D = r.shape
    C = WKV_CHUNK
    lb = min(lane_block, D)
    nlb, nc = D // lb, M // C
    sqm, wdm, tri = (jnp.asarray(m) for m in _wkv_masks())
    fwd = lambda j, c: (c, j)
    bwd = lambda j, c: (nc - 1 - c, j)
    bwd2 = lambda j, c: (nc - 1 - c, nlb + j)
    blk = lambda im: pl.BlockSpec((C, lb), im)
    const = lambda shape: pl.BlockSpec(shape, lambda j, c: (0,) * len(shape))
    est = 24 * _nbytes((C, lb), F32) + 4 * _nbytes(sqm.shape, F32)
    return pl.pallas_call(
        functools.partial(_wkv_kernel, npairs=lb // LANES, nc=nc, Mp=Mp, Lp=Lp, Ls=Ls),
        grid=(nlb, nc),
        in_specs=[blk(fwd), blk(fwd), blk(fwd), blk(fwd), blk(fwd),
                  blk(bwd), blk(bwd), blk(bwd), blk(bwd2), blk(bwd2),
                  pl.BlockSpec((1, lb), lambda j, c: (0, j)), pl.BlockSpec((1, lb), lambda j, c: (0, j)),
                  const(sqm.shape), const(wdm.shape), const(tri.shape)],
        out_specs=[blk(fwd), blk(bwd)],
        out_shape=[jax.ShapeDtypeStruct((M, D), F32)] * 2,
        scratch_shapes=[pltpu.VMEM((2, lb // LANES, LANES, LANES), F32)],
        compiler_params=pltpu.CompilerParams(
            dimension_semantics=("arbitrary", "arbitrary"), vmem_limit_bytes=_vmem_limit(est)),
        name="wkv7_bidir",
    )(r, k, v, lw, a, r, k, v, lw, a, k_k, k_a, sqm, wdm, tri)


def _wkv_post_kernel(yf_ref, yb_ref, r_ref, k_ref, v_ref, a0_ref, a1_ref, g_ref,
                     ka_ref, rk_ref, lg_ref, lb_ref, ones_ref, o_ref):
    ones_bd = ones_ref[...].astype(BF16)
    D = o_ref.shape[1]
    inv_n = 1.0 / RWKV_HEAD
    for p in range(D // LANES):
        sl = slice(p * LANES, (p + 1) * LANES)
        y = yf_ref[:, sl] + yb_ref[:, sl]
        mean = _segsum(y, ones_bd) * inv_n
        dlt = y - mean
        var = _segsum(dlt * dlt, ones_bd) * inv_n
        yn = dlt * lax.rsqrt(var + GN_EPS) * lg_ref[:, sl] + lb_ref[:, sl]
        k, ka = k_ref[:, sl], ka_ref[:, sl]
        k0 = k * (1.0 + (a0_ref[:, sl] - 1.0) * ka)
        k1 = k * (1.0 + (a1_ref[:, sl] - 1.0) * ka)
        kb = 0.5 * k0 + 0.5 * k1
        bonus = _segsum(r_ref[:, sl] * kb * rk_ref[:, sl], ones_bd) * v_ref[:, sl]
        o_ref[:, sl] = ((yn + bonus) * g_ref[:, sl]).astype(o_ref.dtype)


def _wkv_post(yf, yb, r, k, v, a, g, k_a, r_k, lnx_g, lnx_b, *, tm=256):
    M, D = r.shape
    tm = min(tm, M)
    row = pl.BlockSpec((tm, D), lambda i: (i, 0))
    row2 = pl.BlockSpec((tm, D), lambda i: (i, 1))
    vec = pl.BlockSpec((1, D), lambda i: (0, 0))
    P = 2 * RWKV_HEAD
    ones_bd = jnp.asarray((np.arange(P)[:, None] // RWKV_HEAD == np.arange(P)[None, :] // RWKV_HEAD)
                          .astype(np.float32))
    est = 2 * 8 * _nbytes((tm, D), F32) + 2 * _nbytes((tm, D), BF16)
    return pl.pallas_call(
        _wkv_post_kernel,
        grid=(M // tm,),
        in_specs=[row, row, row, row, row, row, row2, row, vec, vec, vec, vec,
                  pl.BlockSpec((P, P), lambda i: (0, 0))],
        out_specs=row,
        out_shape=jax.ShapeDtypeStruct((M, D), BF16),
        compiler_params=pltpu.CompilerParams(
            dimension_semantics=("arbitrary",), vmem_limit_bytes=_vmem_limit(est)),
        name="wkv_post",
    )(yf, yb, r, k, v, a, a, g, k_a, r_k, lnx_g, lnx_b, ones_bd)


N_STAGE = 4


def _norm_permute_kernel(x_ref, g_ref, o_ref, *stage, dil):
    side = ATT_SIDE
    x = x_ref[...]
    inv = lax.rsqrt(jnp.mean(x * x, axis=-1, keepdims=True) + NORM_EPS)
    for j in range(x_ref.shape[1] // LANES):
        sl = slice(j * LANES, (j + 1) * LANES)
        buf = stage[j % len(stage)]
        buf[...] = x_ref[:, sl] * inv * g_ref[:, sl]
        for c in range(dil):
            o_ref[c * side:(c + 1) * side, sl] = buf[pl.ds(c, side, stride=dil), :].astype(o_ref.dtype)


def _norm_class_major(x, g, dil):
    M, D = x.shape
    win = ATT_SIDE * dil
    spec = pl.BlockSpec((win, D), lambda i: (i, 0))
    est = 3 * _nbytes((win, D), F32) + 2 * _nbytes((win, D), BF16) + N_STAGE * _nbytes((win, LANES), F32)
    return pl.pallas_call(
        functools.partial(_norm_permute_kernel, dil=dil),
        grid=(M // win,),
        in_specs=[spec, pl.BlockSpec((1, D), lambda i: (0, 0))],
        out_specs=spec,
        out_shape=jax.ShapeDtypeStruct((M, D), BF16),
        scratch_shapes=[pltpu.VMEM((win, LANES), F32) for _ in range(N_STAGE)],
        compiler_params=pltpu.CompilerParams(
            dimension_semantics=("arbitrary",), vmem_limit_bytes=_vmem_limit(est)),
        name=f"norm_class_major_d{dil}",
    )(x, g)


def _attn_kernel(q_ref, kp_ref, kc_ref, kn_ref, vp_ref, vc_ref, vn_ref, bias_ref, o_ref, lse_ref, *stage,
                 heads, dil, nw_prompt_total, nw_p, nw_s):
    n = pl.program_id(0)
    c = pl.program_id(1)
    side = ATT_SIDE
    in_prompt = n < nw_prompt_total
    nw = jnp.where(in_prompt, nw_p, nw_s)
    pos = lax.rem(n - jnp.where(in_prompt, 0, nw_prompt_total), nw)
    has_prev = pos > 0
    has_next = pos < nw - 1
    i = lax.broadcasted_iota(jnp.int32, (side, 3 * side), 0)
    j = lax.broadcasted_iota(jnp.int32, (side, 3 * side), 1)
    off = j - side - i
    valid = (jnp.abs(off) <= side) & ((j >= side) | has_prev) & ((j < 2 * side) | has_next)
    lane = lax.broadcasted_iota(jnp.int32, (side, LANES), 1)
    hs = [slice(h * ATT_HEAD_DIM, (h + 1) * ATT_HEAD_DIM) for h in range(heads)]
    logits = [_dot(q_ref[:, sl], jnp.concatenate([kp_ref[:, sl], kc_ref[:, sl], kn_ref[:, sl]], axis=0), "nt")
              for sl in hs]
    probs = []
    lse_all = jnp.zeros((side, LANES), F32)
    for h in range(heads):
        s = jnp.where(valid, logits[h] * ATT_SCALE + bias_ref[h], NEG_INF)
        m = jnp.max(s, axis=-1, keepdims=True)
        p = jnp.exp(s - m)
        l = jnp.sum(p, axis=-1, keepdims=True)
        probs.append((p / l).astype(BF16))
        lse_all = jnp.where(lane == h, m + jnp.log(l), lse_all)
    outs = []
    for h, sl in enumerate(hs):
        vcat = jnp.concatenate([vp_ref[:, sl], vc_ref[:, sl], vn_ref[:, sl]], axis=0)
        outs.append(_dot(probs[h], vcat))
    if dil == 1:
        for h, sl in enumerate(hs):
            o_ref[:, sl] = outs[h].astype(o_ref.dtype)
        lse_ref[...] = lse_all
    else:
        rows = pl.ds(c, side, stride=dil)
        for h in range(heads):
            stage[h][rows, :] = outs[h]
        stage[heads][rows, :] = lse_all

        @pl.when(c == dil - 1)
        def _():
            for h, sl in enumerate(hs):
                o_ref[:, sl] = stage[h][...].astype(o_ref.dtype)
            lse_ref[...] = stage[heads][...]


def _dilated_group(qkv, bias, gi, dil, *, Mp, Lp, Ls, heads):
    M, W = qkv.shape
    HD = heads * ATT_HEAD_DIM
    side = ATT_SIDE
    win = side * dil
    nwt = M // win
    cur = lambda t: pl.BlockSpec((side, HD), lambda n, c, t=t: (n * dil + c, t))
    prv = lambda t: pl.BlockSpec((side, HD), lambda n, c, t=t: (jnp.maximum(n - 1, 0) * dil + c, t))
    nxt = lambda t: pl.BlockSpec((side, HD), lambda n, c, t=t: (jnp.minimum(n + 1, nwt - 1) * dil + c, t))
    n_stage = heads + 1 if dil > 1 else 0
    est = 2 * 7 * _nbytes((side, HD), BF16) + 2 * _nbytes(bias.shape, F32) + 2 * _nbytes((win, HD), O_DT) \
        + 2 * _nbytes((win, LANES), F32) + 3 * heads * _nbytes((side, 3 * side), F32) \
        + n_stage * _nbytes((win, LANES), F32)
    return pl.pallas_call(
        functools.partial(_attn_kernel, heads=heads, dil=dil, nw_prompt_total=Mp // win,
                          nw_p=Lp // win, nw_s=Ls // win),
        grid=(nwt, dil),
        in_specs=[cur(0), prv(1), cur(1), nxt(1), prv(2), cur(2), nxt(2),
                  pl.BlockSpec(bias.shape, lambda n, c: (0, 0, 0))],
        out_specs=[pl.BlockSpec((win, HD), lambda n, c: (n, 0)),
                   pl.BlockSpec((win, LANES), lambda n, c: (n, 0))],
        out_shape=[jax.ShapeDtypeStruct((M, HD), O_DT),
                   jax.ShapeDtypeStruct((M, LANES), F32)],
        scratch_shapes=[pltpu.VMEM((win, LANES), F32) for _ in range(n_stage)],
        compiler_params=pltpu.CompilerParams(
            dimension_semantics=("arbitrary", "arbitrary"), vmem_limit_bytes=_vmem_limit(est)),
        name=f"dilated_attn_g{gi}",
    )(qkv, qkv, qkv, qkv, qkv, qkv, qkv, bias)


def _combine_kernel(o0, o1, o2, l0, l1, l2, e_ref, out_ref):
    ls = [l0[...], l1[...], l2[...]]
    m = jnp.maximum(jnp.maximum(ls[0], ls[1]), ls[2])
    es = [jnp.exp(x - m) for x in ls]
    tot = es[0] + es[1] + es[2]
    acc = None
    sel = e_ref[...].astype(BF16)
    for o_ref, e in zip((o0, o1, o2), es):
        w = _segsum(e / tot, sel)
        term = w * o_ref[...].astype(F32)
        acc = term if acc is None else acc + term
    out_ref[...] = acc.astype(out_ref.dtype)


def _combine_groups(os_, lses, *, heads, tm=256):
    M, HD = os_[0].shape
    tm = min(tm, M)
    expand = np.zeros((LANES, HD), np.float32)
    for h in range(heads):
        expand[h, h * ATT_HEAD_DIM:(h + 1) * ATT_HEAD_DIM] = 1.0
    row = pl.BlockSpec((tm, HD), lambda i: (i, 0))
    lrow = pl.BlockSpec((tm, LANES), lambda i: (i, 0))
    est = 2 * 4 * _nbytes((tm, HD), BF16) + 6 * _nbytes((tm, HD), F32)
    return pl.pallas_call(
        _combine_kernel,
        grid=(M // tm,),
        in_specs=[row, row, row, lrow, lrow, lrow, pl.BlockSpec((LANES, HD), lambda i: (0, 0))],
        out_specs=row,
        out_shape=jax.ShapeDtypeStruct((M, HD), BF16),
        compiler_params=pltpu.CompilerParams(
            dimension_semantics=("arbitrary",), vmem_limit_bytes=_vmem_limit(est)),
        name="attn_combine",
    )(*os_, *lses, jnp.asarray(expand))


def _t5_bucket(rel):
    half = NUM_BUCKETS // 2
    max_exact = half // 2
    ret = jnp.where(rel > 0, half, 0)
    n = jnp.abs(rel)
    nf = jnp.maximum(n, 1).astype(F32)
    large = max_exact + (jnp.log(nf / max_exact) / math.log(MAX_DISTANCE / max_exact)
                         * (half - max_exact)).astype(jnp.int32)
    large = jnp.minimum(large, half - 1)
    return ret + jnp.where(n < max_exact, n, large)


def _rel_bias(table_g, dil, side):
    i = jnp.arange(side)[:, None]
    j = jnp.arange(3 * side)[None, :]
    rel = dil * (j - side - i)
    return jnp.transpose(table_g[_t5_bucket(rel)], (2, 0, 1)).astype(F32)


def _ep_cast(accs, rows, vecs):
    return tuple(accs)


def _ep_lora1(accs, rows, vecs):
    return jnp.tanh(accs[0]), accs[1], jax.nn.sigmoid(accs[2])


def _ep_lora2(accs, rows, vecs):
    lw = -math.exp(-0.5) * jax.nn.sigmoid(vecs[0] + accs[0])
    return lw, jax.nn.sigmoid(vecs[1] + accs[1])


def _ep_residual_norm(accs, rows, vecs):
    x_new = rows[0] + _rms(accs[0], vecs[0])
    return x_new, _rms(x_new, vecs[1])


def _ep_swiglu(accs, rows, vecs):
    return (jax.nn.silu(accs[0]) * accs[1],)


def _ep_ple(accs, rows, vecs):
    return (rows[0] + jax.nn.sigmoid(accs[0]) * accs[1],)


def _ep_ple_norm(accs, rows, vecs):
    x_new = rows[0] + jax.nn.sigmoid(accs[0]) * accs[1]
    return x_new, _rms(x_new, vecs[0])


def _pad_cols(w, n):
    return jnp.pad(w, ((0, 0), (0, n - w.shape[1])))


def _block_rows(w0, w1, n):
    h = n // 2
    z0 = jnp.zeros_like(w0)
    top = jnp.pad(jnp.concatenate([w0, z0], axis=1), ((0, h - w0.shape[0]), (0, 0)))
    bot = jnp.pad(jnp.concatenate([z0, w1], axis=1), ((0, h - w1.shape[0]), (0, 0)))
    return jnp.concatenate([top, bot], axis=0)


def _rwkv_layer(x, g_in, g_out, g_next, p, *, Mp, Lp, Ls):
    M, D = x.shape
    (mu, w_rkv, w0, w1, w2, a0, a1, a2, g1, g2, k_k, k_a, r_k, lnx_g, lnx_b, w_o) = p
    xs = _shift_mix(x, g_in[None], mu, Mp=Mp, Lp=Lp, Ls=Ls)
    wr, wk, wv = (w_rkv[i].astype(BF16) for i in range(3))
    r, = _fused_mm([(xs[0], wr, 0)], [], [], _ep_cast, [F32], D, tm=512, tn=1024, name="proj_r")
    k, = _fused_mm([(xs[1], wk, 0)], [], [], _ep_cast, [F32], D, tm=512, tn=1024, name="proj_k")
    v, = _fused_mm([(xs[2], wv, 0)], [], [], _ep_cast, [F32], D, tm=512, tn=1024, name="proj_v")

    LP = 2 * LANES
    half = LP // 2
    w1c = jnp.concatenate([_pad_cols(w1[0], half), _pad_cols(w1[1], half)], axis=1).astype(BF16)
    a1c = jnp.concatenate([_pad_cols(a1[0], half), _pad_cols(a1[1], half)], axis=1).astype(BF16)
    g1c = _pad_cols(g1, LP).astype(BF16)
    tw, ta, sg = _fused_mm([(xs[3], w1c, 0), (xs[4], a1c, 0), (xs[5], g1c, 0)], [], [], _ep_lora1,
                           [BF16, BF16, BF16], LP, tm=512, tn=LP, name="lora_in")
    w2c = _block_rows(w2[0], w2[1], LP).astype(BF16)
    a2c = _block_rows(a2[0], a2[1], LP).astype(BF16)
    w0c = jnp.concatenate([w0[0], w0[1]])[None]
    a0c = jnp.concatenate([a0[0], a0[1]])[None]
    lw, a = _fused_mm([(tw, w2c, 0), (ta, a2c, 0)], [], [w0c, a0c], _ep_lora2, [F32, F32], 2 * D,
                      tm=512, tn=1024, name="lora_out_wa")
    g2c = jnp.pad(g2, ((0, LP - g2.shape[0]), (0, 0))).astype(BF16)
    g, = _fused_mm([(sg, g2c, 0)], [], [], _ep_cast, [F32], D, tm=512, tn=1024, name="lora_out_g")

    yf, yb = _wkv(r, k, v, lw, a, k_k[None], k_a[None], Mp=Mp, Lp=Lp, Ls=Ls)
    o = _wkv_post(yf, yb, r, k, v, a, g, k_a[None], r_k.reshape(1, D), lnx_g[None], lnx_b[None])
    return _fused_mm([(o, w_o.astype(BF16), 0)], [x], [g_out[None], g_next[None]], _ep_residual_norm,
                     [F32, BF16], D, tm=256, tn=D, name="rwkv_out")


def _attn_layer(x, xn, g_in, g_out, g_next, w_qkv, w_o, rel_table, *, Mp, Lp, Ls):
    M, D = x.shape
    heads = D // ATT_HEAD_DIM
    Wg = 3 * heads * ATT_HEAD_DIM
    w_bf = w_qkv.astype(BF16)
    os_, lses = [], []
    for gi, (win, dil) in enumerate(DILATED_GROUPS):
        side = win // (2 * dil)
        assert side == ATT_SIDE and Lp % (side * dil) == 0 and Ls % (side * dil) == 0
        xg = xn if dil == 1 else _norm_class_major(x, g_in[None], dil)
        qkv, = _fused_mm([(xg, w_bf, gi * Wg)], [], [], _ep_cast, [BF16], Wg, tm=512, tn=1024,
                         name=f"qkv_g{gi}")
        bias = _rel_bias(rel_table[:, gi * heads:(gi + 1) * heads], dil, side)
        o, lse = _dilated_group(qkv, bias, gi, dil, Mp=Mp, Lp=Lp, Ls=Ls, heads=heads)
        os_.append(o)
        lses.append(lse)
    o = _combine_groups(os_, lses, heads=heads)
    return _fused_mm([(o, w_o.astype(BF16), 0)], [x], [g_out[None], g_next[None]], _ep_residual_norm,
                     [F32, BF16], D, tm=256, tn=D, name="attn_out")


def _ffn_ple(x, xn, g_out, ple_norm, w_gu, w_down, p, w_gate, w_proj, g_next):
    M, D = x.shape
    F = w_down.shape[0]
    wgu = w_gu.astype(BF16)
    hmid, = _fused_mm([(xn, wgu, 0), (xn, wgu, F)], [], [], _ep_swiglu, [BF16], F,
                      tm=512, tn=512, name="ffn_up")
    x, xn = _fused_mm([(hmid, w_down.astype(BF16), 0)], [x], [g_out[None], ple_norm[None]],
                      _ep_residual_norm, [F32, BF16], D, tm=256, tn=D, tk=512, name="ffn_down")
    dots = [(xn, w_gate.astype(BF16), 0), (p, w_proj.astype(BF16), 0)]
    if g_next is None:
        x, = _fused_mm(dots, [x], [], _ep_ple, [F32], D, tm=256, tn=D, name="ple_last")
        return x, None
    return _fused_mm(dots, [x], [g_next[None]], _ep_ple_norm, [F32, BF16], D, tm=256, tn=D, name="ple")


def _first_norm_kernel(x_ref, g_ref, o_ref):
    o_ref[...] = _rms(x_ref[...], g_ref[...]).astype(o_ref.dtype)


def _first_norm(x, g, *, tm=512):
    M, D = x.shape
    tm = min(tm, M)
    return pl.pallas_call(
        _first_norm_kernel,
        grid=(M // tm,),
        in_specs=[pl.BlockSpec((tm, D), lambda i: (i, 0)), pl.BlockSpec((1, D), lambda i: (0, 0))],
        out_specs=pl.BlockSpec((tm, D), lambda i: (i, 0)),
        out_shape=jax.ShapeDtypeStruct((M, D), BF16),
        compiler_params=pltpu.CompilerParams(dimension_semantics=("arbitrary",)),
        name="first_norm",
    )(x, g[None])


def kernel(x_prompt, x_sample, p_prompt, p_sample, norm_gains, rel_bias_table, rwkv_mu, rwkv_w_rkv, rwkv_w0, rwkv_w1, rwkv_w2, rwkv_a0, rwkv_a1, rwkv_a2, rwkv_g1, rwkv_g2, rwkv_k_k, rwkv_k_a, rwkv_r_k, rwkv_lnx_g, rwkv_lnx_b, rwkv_w_o, att_w_qkv, att_w_o, ffn_w_gu, ffn_w_down, ple_norm, ple_w_gate, ple_w_proj):
    Bp, Lp, D = x_prompt.shape
    Bs, Ls, _ = x_sample.shape
    Mp, Ms = Bp * Lp, Bs * Ls
    depth = norm_gains.shape[0]
    x = jnp.concatenate([x_prompt.reshape(Mp, D), x_sample.reshape(Ms, D)], axis=0)
    pdim = p_prompt.shape[-1]
    p = jnp.concatenate([p_prompt.reshape(depth, Mp, pdim), p_sample.reshape(depth, Ms, pdim)],
                        axis=1).astype(BF16)
    seq = dict(Mp=Mp, Lp=Lp, Ls=Ls)
    xn = None
    for i in range(depth):
        g = norm_gains[i]
        j = i // 2
        if i % 2 == 0:
            params = (rwkv_mu[j], rwkv_w_rkv[j], rwkv_w0[j], rwkv_w1[j], rwkv_w2[j], rwkv_a0[j], rwkv_a1[j],
                      rwkv_a2[j], rwkv_g1[j], rwkv_g2[j], rwkv_k_k[j], rwkv_k_a[j], rwkv_r_k[j],
                      rwkv_lnx_g[j], rwkv_lnx_b[j], rwkv_w_o[j])
            x, xn = _rwkv_layer(x, g[0], g[1], g[2], params, **seq)
        else:
            if xn is None:
                xn = _first_norm(x, g[0])
            x, xn = _attn_layer(x, xn, g[0], g[1], g[2], att_w_qkv[j], att_w_o[j], rel_bias_table, **seq)
        g_next = norm_gains[i + 1][0] if (i + 1 < depth and (i + 1) % 2 == 1) else None
        x, xn = _ffn_ple(x, xn, g[3], ple_norm[i], ffn_w_gu[i], ffn_w_down[i], p[i],
                         ple_w_gate[i], ple_w_proj[i], g_next)
    return x[:Mp].reshape(Bp, Lp, D), x[Mp:].reshape(Bs, Ls, D)
```

```python
import functools
import math

import numpy as np
import jax
import jax.numpy as jnp
from jax import lax
from jax.experimental import pallas as pl
from jax.experimental.pallas import tpu as pltpu

F32 = jnp.float32
BF16 = jnp.bfloat16

NORM_EPS = 1e-6
GN_EPS = 64e-5
RWKV_HEAD = 64
ATT_HEAD_DIM = 128
DILATED_GROUPS = ((128, 1), (512, 4), (2048, 16))
ATT_SCALE = 1.0 / math.sqrt(ATT_HEAD_DIM)
NEG_INF = -1e30
NUM_BUCKETS = 32
MAX_DISTANCE = 1024

LANES = 128
V7X_VMEM_BYTES = 64 * 1024 * 1024
VMEM_CAP = V7X_VMEM_BYTES - 8 * 1024 * 1024

WKV_CHUNK = 64
WKV_MERGE_BLOCKS = (1, 2, 4, 8, 16, 32)
ATT_SIDE = 64
O_DT = jnp.bfloat16


def _vmem_limit(est_bytes):
    return int(min(max(est_bytes + est_bytes // 2 + 8 * 1024 * 1024, 32 * 1024 * 1024), VMEM_CAP))


def _nbytes(shape, dtype):
    return int(np.prod(shape)) * jnp.dtype(dtype).itemsize


_DN = {"nn": (((1,), (0,)), ((), ())), "nt": (((1,), (1,)), ((), ())), "tn": (((0,), (0,)), ((), ()))}


def _dot(a, b, dims="nn", f32=False):
    if f32:
        return lax.dot_general(a.astype(F32), b.astype(F32), _DN[dims],
                               precision=lax.Precision.HIGHEST, preferred_element_type=F32)
    return lax.dot_general(a.astype(BF16), b.astype(BF16), _DN[dims], preferred_element_type=F32)


def _segsum(x, sel):
    hi = x.astype(BF16)
    lo = (x - hi.astype(F32)).astype(BF16)
    return _dot(hi, sel) + _dot(lo, sel)


def _rms(x, g):
    return x * lax.rsqrt(jnp.mean(x * x, axis=-1, keepdims=True) + NORM_EPS) * g


def _fused_mm_kernel(*refs, nd, nr, nv, no, nk, epilogue):
    d = refs[:2 * nd]
    r = refs[2 * nd:2 * nd + nr]
    v = refs[2 * nd + nr:2 * nd + nr + nv]
    o = refs[2 * nd + nr + nv:2 * nd + nr + nv + no]
    scratch = refs[2 * nd + nr + nv + no:]

    def finish(accs):
        outs = epilogue(accs, [x[...] for x in r], [x[...] for x in v])
        for o_ref, val in zip(o, outs):
            o_ref[...] = val.astype(o_ref.dtype)

    if nk == 1:
        finish([jnp.dot(d[2 * t][...], d[2 * t + 1][...], preferred_element_type=F32) for t in range(nd)])
    else:
        acc_ref = scratch[0]
        k = pl.program_id(2)

        @pl.when(k == 0)
        def _():
            acc_ref[...] = jnp.zeros_like(acc_ref)

        acc_ref[...] += jnp.dot(d[0][...], d[1][...], preferred_element_type=F32)

        @pl.when(k == nk - 1)
        def _():
            finish([acc_ref[...]])


def _fused_mm(dots, rows, vecs, epilogue, out_dtypes, n_cols, *, tm, tn, tk=None, name):
    M = dots[0][0].shape[0]
    tm = min(tm, M)
    tn = max(t for t in range(LANES, min(tn, n_cols) + 1, LANES)
             if n_cols % t == 0 and all(c0 % t == 0 for _, _, c0 in dots))
    assert M % tm == 0 and n_cols % tn == 0
    dots = [(x, w, c0 // tn) for x, w, c0 in dots]
    nd, nr, nv, no = len(dots), len(rows), len(vecs), len(out_dtypes)
    K0 = dots[0][0].shape[1]
    nk = 1
    if tk is not None and K0 > tk:
        assert nd == 1 and K0 % tk == 0
        nk = K0 // tk
    grid = (M // tm, n_cols // tn) + ((nk,) if nk > 1 else ())
    w_resident = nk == 1 and n_cols == tn
    in_specs = []
    est = 0
    for x, w, off in dots:
        K = x.shape[1]
        kb = tk if nk > 1 else K
        if nk > 1:
            in_specs.append(pl.BlockSpec((tm, kb), lambda i, j, k: (i, k)))
            in_specs.append(pl.BlockSpec((kb, tn), lambda i, j, k, off=off: (k, j + off)))
        elif w_resident:
            in_specs.append(pl.BlockSpec((tm, kb), lambda i, j: (i, 0)))
            in_specs.append(pl.BlockSpec((kb, tn), lambda i, j, off=off: (0, j + off),
                                         pipeline_mode=pl.Buffered(1)))
        else:
            in_specs.append(pl.BlockSpec((tm, kb), lambda i, j: (i, 0)))
            in_specs.append(pl.BlockSpec((kb, tn), lambda i, j, off=off: (0, j + off)))
        est += 2 * _nbytes((tm, kb), x.dtype) + (1 if w_resident else 2) * _nbytes((kb, tn), w.dtype) \
            + 2 * _nbytes((tm, tn), F32)
    for a in rows:
        in_specs.append(pl.BlockSpec((tm, tn), lambda i, j, *k: (i, j)))
        est += 2 * _nbytes((tm, tn), a.dtype)
    for a in vecs:
        in_specs.append(pl.BlockSpec((1, tn), lambda i, j, *k: (0, j)))
    out_specs = [pl.BlockSpec((tm, tn), lambda i, j, *k: (i, j)) for _ in out_dtypes]
    est += sum(2 * _nbytes((tm, tn), dt) for dt in out_dtypes)
    scratch = [pltpu.VMEM((tm, tn), F32)] if nk > 1 else []
    args = []
    for x, w, _ in dots:
        args += [x, w]
    args += list(rows) + list(vecs)
    outs = pl.pallas_call(
        functools.partial(_fused_mm_kernel, nd=nd, nr=nr, nv=nv, no=no, nk=nk, epilogue=epilogue),
        grid=grid,
        in_specs=in_specs,
        out_specs=out_specs,
        out_shape=[jax.ShapeDtypeStruct((M, n_cols), dt) for dt in out_dtypes],
        scratch_shapes=scratch,
        compiler_params=pltpu.CompilerParams(
            dimension_semantics=("arbitrary",) * len(grid), vmem_limit_bytes=_vmem_limit(est)),
        name=name,
    )(*args)
    return outs


def _seq_pos(r0, Mp, Lp, Ls):
    in_prompt = r0 < Mp
    L = jnp.where(in_prompt, Lp, Ls)
    pos = lax.rem(r0 - jnp.where(in_prompt, 0, Mp), L)
    return pos, L


def _shift_mix_kernel(x_ref, xp_ref, xn_ref, g_ref, mu_ref, *outs, tm, Mp, Lp, Ls):
    r0 = pl.program_id(0) * tm
    pos, L = _seq_pos(r0, Mp, Lp, Ls)
    g = g_ref[...]
    h = _rms(x_ref[...], g)
    hp = _rms(xp_ref[...], g)[7:8]
    hn = _rms(xn_ref[...], g)[0:1]
    hp = jnp.where(pos == 0, 0.0, hp)
    hn = jnp.where(pos + tm == L, 0.0, hn)
    row = lax.broadcasted_iota(jnp.int32, (tm, 1), 0)
    prev = jnp.where(row == 0, hp, pltpu.roll(h, 1, 0))
    nxt = jnp.where(row == tm - 1, hn, pltpu.roll(h, tm - 1, 0))
    xx = 0.5 * (prev + nxt) - h
    for j, o_ref in enumerate(outs):
        o_ref[...] = (h + xx * mu_ref[j:j + 1, :]).astype(o_ref.dtype)


def _shift_mix(x, g, mu, *, Mp, Lp, Ls, tm=256):
    M, D = x.shape
    tm = min(tm, Lp, Ls)
    nb8 = M // 8
    est = 2 * _nbytes((tm, D), F32) + 12 * _nbytes((tm, D), BF16) + 6 * _nbytes((tm, D), F32)
    return pl.pallas_call(
        functools.partial(_shift_mix_kernel, tm=tm, Mp=Mp, Lp=Lp, Ls=Ls),
        grid=(M // tm,),
        in_specs=[
            pl.BlockSpec((tm, D), lambda i: (i, 0)),
            pl.BlockSpec((8, D), lambda i: (jnp.maximum(i * (tm // 8) - 1, 0), 0)),
            pl.BlockSpec((8, D), lambda i: (jnp.minimum((i + 1) * (tm // 8), nb8 - 1), 0)),
            pl.BlockSpec((1, D), lambda i: (0, 0)),
            pl.BlockSpec((6, D), lambda i: (0, 0)),
        ],
        out_specs=[pl.BlockSpec((tm, D), lambda i: (i, 0)) for _ in range(6)],
        out_shape=[jax.ShapeDtypeStruct((M, D), BF16) for _ in range(6)],
        compiler_params=pltpu.CompilerParams(
            dimension_semantics=("arbitrary",), vmem_limit_bytes=_vmem_limit(est)),
        name="shift_mix",
    )(x, x, x, g, mu)


def _wkv_masks():
    C = WKV_CHUNK
    P = 2 * C
    t = np.arange(P)[:, None]
    s = np.arange(P)[None, :]
    bd = (t // C) == (s // C)
    tw = np.arange(C)[:, None]
    sw = np.arange(P)[None, :] % C
    sq, wide, tri = [], [], []
    for rev in (False, True):
        ms = [bd]
        for b in WKV_MERGE_BLOCKS:
            same = (t // (2 * b)) == (s // (2 * b))
            hi_t, hi_s = (t // b) % 2, (s // b) % 2
            ms.append(same & (hi_t == (0 if rev else 1)) & (hi_s == (1 if rev else 0)))
        sq.append(np.stack(ms))
        strict = (sw < tw) if not rev else (sw > tw)
        incl = (sw <= tw) if not rev else (sw >= tw)
        wide.append(np.stack([strict, incl]))
        tc, sc = np.arange(C)[:, None], np.arange(C)[None, :]
        tri.append((sc <= tc) if not rev else (sc >= tc))
    return (np.stack(sq).astype(np.float32), np.stack(wide).astype(np.float32),
            np.stack(tri).astype(np.float32))


def _split3(x):
    hi = x.astype(BF16)
    r1 = x - hi.astype(F32)
    mid = r1.astype(BF16)
    lo = (r1 - mid.astype(F32)).astype(BF16)
    return hi, mid, lo


def _wkv_kernel(rf, kf, vf, lwf, af, rb_, kb_, vb_, lwb, ab, kk_ref, ka_ref, sqm_ref, wdm_ref, tri_ref,
                yf_ref, yb_ref, s_ref, *, npairs, nc, Mp, Lp, Ls):
    c = pl.program_id(1)
    C = WKV_CHUNK
    pos_f, _ = _seq_pos(c * C, Mp, Lp, Ls)
    pos_b, L_b = _seq_pos((nc - 1 - c) * C, Mp, Lp, Ls)

    @pl.when(pos_f == 0)
    def _():
        s_ref[0] = jnp.zeros_like(s_ref[0])

    @pl.when(pos_b + C == L_b)
    def _():
        s_ref[1] = jnp.zeros_like(s_ref[1])

    ones_bd = sqm_ref[0, 0].astype(BF16)
    bd = sqm_ref[0, 0] > 0
    eye = (lax.broadcasted_iota(jnp.int32, (2 * C, 2 * C), 0)
           == lax.broadcasted_iota(jnp.int32, (2 * C, 2 * C), 1)).astype(F32)
    dirs = ((rf, kf, vf, lwf, af, yf_ref), (rb_, kb_, vb_, lwb, ab, yb_ref))
    units = [(d, p) for d in range(2) for p in range(npairs)]
    sls = [slice(p * LANES, (p + 1) * LANES) for p in range(npairs)]

    def stack_bd(x):
        return jnp.where(bd, jnp.concatenate([x, x], axis=0), 0.0)

    G_dir = []
    for d in range(2):
        lw = dirs[d][3][...]
        hi, mid, lo = _split3(lw)
        tri = tri_ref[d].astype(BF16)
        G_dir.append(_dot(tri, hi) + (_dot(tri, mid) + _dot(tri, lo)))

    kr = [dirs[d][1][:, sls[p]] * kk_ref[:, sls[p]] for d, p in units]
    ss = _segsum(jnp.concatenate([x * x for x in kr], axis=0), ones_bd)

    st = []
    for i, (d, p) in enumerate(units):
        r_ref, k_ref, v_ref, lw_ref, a_ref, _ = dirs[d]
        sl = sls[p]
        kk = kr[i] * lax.rsqrt(ss[i * C:(i + 1) * C] + 1e-12)
        lw, a, k = lw_ref[:, sl], a_ref[:, sl], k_ref[:, sl]
        G = G_dir[d][:, sl]
        g_end = jnp.sum(lw, axis=0, keepdims=True)
        to_end = jnp.exp(g_end - G)
        inv_gam = jnp.exp(-G)
        kj = k * (1.0 + (a - 1.0) * ka_ref[:, sl])
        beta = kk * a
        u = dict(v_st=stack_bd(v_ref[:, sl]), v=v_ref[:, sl], decay=jnp.exp(g_end),
                 bk=jnp.concatenate([beta * to_end, kj * to_end], axis=0))
        u["lhs"] = jnp.concatenate([-kk * jnp.exp(G - lw), r_ref[:, sl] * jnp.exp(G)], axis=0)
        u["rhs"] = jnp.concatenate([stack_bd(beta * inv_gam), stack_bd(kj * inv_gam)], axis=0)
        st.append(u)

    for u in st:
        u["x1"] = _dot(u["lhs"], u.pop("rhs"), "nt")
    for u, (d, p) in zip(st, units):
        strict, incl = wdm_ref[d, 0] > 0, wdm_ref[d, 1] > 0
        x1 = u.pop("x1")
        a_ab = jnp.where(strict, x1[:C, :LANES], 0.0)
        u["a_ak"] = jnp.where(strict, x1[:C, LANES:], 0.0)
        u["m_rb"] = jnp.where(incl, x1[C:, :LANES], 0.0)
        u["m_rk"] = jnp.where(incl, x1[C:, LANES:], 0.0)
        u["aa"] = jnp.concatenate([a_ab, a_ab], axis=0)
    for u, (d, p) in zip(st, units):
        u["tinv"] = eye + u["aa"] * sqm_ref[d, 1]
    for lvl in range(2, len(WKV_MERGE_BLOCKS) + 1):
        for u, (d, p) in zip(st, units):
            u["m"] = _dot(u["aa"] * sqm_ref[d, lvl], u["tinv"])
        for u in st:
            u["tinv"] = u["tinv"] + _dot(u["tinv"], u.pop("m"))
    for u in st:
        xyv = _dot(jnp.concatenate([u["a_ak"], u["m_rk"]], axis=0), u["v_st"])
        u["xv"], u["yv"] = xyv[:C], xyv[C:]
        tinv = u.pop("tinv")
        u["tw"] = tinv[:C] + tinv[C:]

    S = [s_ref[d, p] for d, p in units]
    xy = [_dot(u["lhs"], s, "nt") for u, s in zip(st, S)]
    xs = [z[:C] + u["xv"] for u, z in zip(st, xy)]
    ys = [z[C:] + u["yv"] for u, z in zip(st, xy)]
    us = [_dot(u["tw"], stack_bd(x)) for u, x in zip(st, xs)]
    for i, (d, p) in enumerate(units):
        u = st[i]
        dirs[d][5][:, sls[p]] = ys[i] + _dot(u["m_rb"], stack_bd(us[i]))
        uv = jnp.concatenate([us[i], u["v"]], axis=0)
        s_ref[d, p] = jnp.where(bd, S[i] * u["decay"] + _dot(uv, u["bk"], "tn"), 0.0)


def _wkv(r, k, v, lw, a, k_k, k_a, *, Mp, Lp, Ls, lane_block=512):
    M, D = r.shape
    C = WKV_CHUNK
    lb = min(lane_block, D)
    nlb, nc = D // lb, M // C
    sqm, wdm, tri = (jnp.asarray(m) for m in _wkv_masks())
    fwd = lambda j, c: (c, j)
    bwd = lambda j, c: (nc - 1 - c, j)
    bwd2 = lambda j, c: (nc - 1 - c, nlb + j)
    blk = lambda im: pl.BlockSpec((C, lb), im)
    const = lambda shape: pl.BlockSpec(shape, lambda j, c: (0,) * len(shape))
    est = 24 * _nbytes((C, lb), F32) + 4 * _nbytes(sqm.shape, F32)
    return pl.pallas_call(
        functools.partial(_wkv_kernel, npairs=lb // LANES, nc=nc, Mp=Mp, Lp=Lp, Ls=Ls),
        grid=(nlb, nc),
        in_specs=[blk(fwd), blk(fwd), blk(fwd), blk(fwd), blk(fwd),
                  blk(bwd), blk(bwd), blk(bwd), blk(bwd2), blk(bwd2),
                  pl.BlockSpec((1, lb), lambda j, c: (0, j)), pl.BlockSpec((1, lb), lambda j, c: (0, j)),
                  const(sqm.shape), const(wdm.shape), const(tri.shape)],
        out_specs=[blk(fwd), blk(bwd)],
        out_shape=[jax.ShapeDtypeStruct((M, D), F32)] * 2,
        scratch_shapes=[pltpu.VMEM((2, lb // LANES, LANES, LANES), F32)],
        compiler_params=pltpu.CompilerParams(
            dimension_semantics=("arbitrary", "arbitrary"), vmem_limit_bytes=_vmem_limit(est)),
        name="wkv7_bidir",
    )(r, k, v, lw, a, r, k, v, lw, a, k_k, k_a, sqm, wdm, tri)


def _wkv_post_kernel(yf_ref, yb_ref, r_ref, k_ref, v_ref, a0_ref, a1_ref, g_ref,
                     ka_ref, rk_ref, lg_ref, lb_ref, ones_ref, o_ref):
    ones_bd = ones_ref[...].astype(BF16)
    D = o_ref.shape[1]
    inv_n = 1.0 / RWKV_HEAD
    for p in range(D // LANES):
        sl = slice(p * LANES, (p + 1) * LANES)
        y = yf_ref[:, sl] + yb_ref[:, sl]
        mean = _segsum(y, ones_bd) * inv_n
        dlt = y - mean
        var = _segsum(dlt * dlt, ones_bd) * inv_n
        yn = dlt * lax.rsqrt(var + GN_EPS) * lg_ref[:, sl] + lb_ref[:, sl]
        k, ka = k_ref[:, sl], ka_ref[:, sl]
        k0 = k * (1.0 + (a0_ref[:, sl] - 1.0) * ka)
        k1 = k * (1.0 + (a1_ref[:, sl] - 1.0) * ka)
        kb = 0.5 * k0 + 0.5 * k1
        bonus = _segsum(r_ref[:, sl] * kb * rk_ref[:, sl], ones_bd) * v_ref[:, sl]
        o_ref[:, sl] = ((yn + bonus) * g_ref[:, sl]).astype(o_ref.dtype)


def _wkv_post(yf, yb, r, k, v, a, g, k_a, r_k, lnx_g, lnx_b, *, tm=256):
    M, D = r.shape
    tm = min(tm, M)
    row = pl.BlockSpec((tm, D), lambda i: (i, 0))
    row2 = pl.BlockSpec((tm, D), lambda i: (i, 1))
    vec = pl.BlockSpec((1, D), lambda i: (0, 0))
    P = 2 * RWKV_HEAD
    ones_bd = jnp.asarray((np.arange(P)[:, None] // RWKV_HEAD == np.arange(P)[None, :] // RWKV_HEAD)
                          .astype(np.float32))
    est = 2 * 8 * _nbytes((tm, D), F32) + 2 * _nbytes((tm, D), BF16)
    return pl.pallas_call(
        _wkv_post_kernel,
        grid=(M // tm,),
        in_specs=[row, row, row, row, row, row, row2, row, vec, vec, vec, vec,
                  pl.BlockSpec((P, P), lambda i: (0, 0))],
        out_specs=row,
        out_shape=jax.ShapeDtypeStruct((M, D), BF16),
        compiler_params=pltpu.CompilerParams(
            dimension_semantics=("arbitrary",), vmem_limit_bytes=_vmem_limit(est)),
        name="wkv_post",
    )(yf, yb, r, k, v, a, a, g, k_a, r_k, lnx_g, lnx_b, ones_bd)


N_STAGE = 4


def _norm_permute_kernel(x_ref, g_ref, o_ref, *stage, dil):
    side = ATT_SIDE
    x = x_ref[...]
    inv = lax.rsqrt(jnp.mean(x * x, axis=-1, keepdims=True) + NORM_EPS)
    for j in range(x_ref.shape[1] // LANES):
        sl = slice(j * LANES, (j + 1) * LANES)
        buf = stage[j % len(stage)]
        buf[...] = x_ref[:, sl] * inv * g_ref[:, sl]
        for c in range(dil):
            o_ref[c * side:(c + 1) * side, sl] = buf[pl.ds(c, side, stride=dil), :].astype(o_ref.dtype)


def _norm_class_major(x, g, dil):
    M, D = x.shape
    win = ATT_SIDE * dil
    spec = pl.BlockSpec((win, D), lambda i: (i, 0))
    est = 3 * _nbytes((win, D), F32) + 2 * _nbytes((win, D), BF16) + N_STAGE * _nbytes((win, LANES), F32)
    return pl.pallas_call(
        functools.partial(_norm_permute_kernel, dil=dil),
        grid=(M // win,),
        in_specs=[spec, pl.BlockSpec((1, D), lambda i: (0, 0))],
        out_specs=spec,
        out_shape=jax.ShapeDtypeStruct((M, D), BF16),
        scratch_shapes=[pltpu.VMEM((win, LANES), F32) for _ in range(N_STAGE)],
        compiler_params=pltpu.CompilerParams(
            dimension_semantics=("arbitrary",), vmem_limit_bytes=_vmem_limit(est)),
        name=f"norm_class_major_d{dil}",
    )(x, g)


def _attn_kernel(q_ref, kp_ref, kc_ref, kn_ref, vp_ref, vc_ref, vn_ref, bias_ref, o_ref, lse_ref, *stage,
                 heads, dil, nw_prompt_total, nw_p, nw_s):
    n = pl.program_id(0)
    c = pl.program_id(1)
    side = ATT_SIDE
    in_prompt = n < nw_prompt_total
    nw = jnp.where(in_prompt, nw_p, nw_s)
    pos = lax.rem(n - jnp.where(in_prompt, 0, nw_prompt_total), nw)
    has_prev = pos > 0
    has_next = pos < nw - 1
    i = lax.broadcasted_iota(jnp.int32, (side, 3 * side), 0)
    j = lax.broadcasted_iota(jnp.int32, (side, 3 * side), 1)
    off = j - side - i
    valid = (jnp.abs(off) <= side) & ((j >= side) | has_prev) & ((j < 2 * side) | has_next)
    lane = lax.broadcasted_iota(jnp.int32, (side, LANES), 1)
    hs = [slice(h * ATT_HEAD_DIM, (h + 1) * ATT_HEAD_DIM) for h in range(heads)]
    logits = [_dot(q_ref[:, sl], jnp.concatenate([kp_ref[:, sl], kc_ref[:, sl], kn_ref[:, sl]], axis=0), "nt")
              for sl in hs]
    probs = []
    lse_all = jnp.zeros((side, LANES), F32)
    for h in range(heads):
        s = jnp.where(valid, logits[h] * ATT_SCALE + bias_ref[h], NEG_INF)
        m = jnp.max(s, axis=-1, keepdims=True)
        p = jnp.exp(s - m)
        l = jnp.sum(p, axis=-1, keepdims=True)
        probs.append((p / l).astype(BF16))
        lse_all = jnp.where(lane == h, m + jnp.log(l), lse_all)
    outs = []
    for h, sl in enumerate(hs):
        vcat = jnp.concatenate([vp_ref[:, sl], vc_ref[:, sl], vn_ref[:, sl]], axis=0)
        outs.append(_dot(probs[h], vcat))
    if dil == 1:
        for h, sl in enumerate(hs):
            o_ref[:, sl] = outs[h].astype(o_ref.dtype)
        lse_ref[...] = lse_all
    else:
        rows = pl.ds(c, side, stride=dil)
        for h in range(heads):
            stage[h][rows, :] = outs[h]
        stage[heads][rows, :] = lse_all

        @pl.when(c == dil - 1)
        def _():
            for h, sl in enumerate(hs):
                o_ref[:, sl] = stage[h][...].astype(o_ref.dtype)
            lse_ref[...] = stage[heads][...]


def _dilated_group(qkv, bias, gi, dil, *, Mp, Lp, Ls, heads):
    M, W = qkv.shape
    HD = heads * ATT_HEAD_DIM
    side = ATT_SIDE
    win = side * dil
    nwt = M // win
    cur = lambda t: pl.BlockSpec((side, HD), lambda n, c, t=t: (n * dil + c, t))
    prv = lambda t: pl.BlockSpec((side, HD), lambda n, c, t=t: (jnp.maximum(n - 1, 0) * dil + c, t))
    nxt = lambda t: pl.BlockSpec((side, HD), lambda n, c, t=t: (jnp.minimum(n + 1, nwt - 1) * dil + c, t))
    n_stage = heads + 1 if dil > 1 else 0
    est = 2 * 7 * _nbytes((side, HD), BF16) + 2 * _nbytes(bias.shape, F32) + 2 * _nbytes((win, HD), O_DT) \
        + 2 * _nbytes((win, LANES), F32) + 3 * heads * _nbytes((side, 3 * side), F32) \
        + n_stage * _nbytes((win, LANES), F32)
    return pl.pallas_call(
        functools.partial(_attn_kernel, heads=heads, dil=dil, nw_prompt_total=Mp // win,
                          nw_p=Lp // win, nw_s=Ls // win),
        grid=(nwt, dil),
        in_specs=[cur(0), prv(1), cur(1), nxt(1), prv(2), cur(2), nxt(2),
                  pl.BlockSpec(bias.shape, lambda n, c: (0, 0, 0))],
        out_specs=[pl.BlockSpec((win, HD), lambda n, c: (n, 0)),
                   pl.BlockSpec((win, LANES), lambda n, c: (n, 0))],
        out_shape=[jax.ShapeDtypeStruct((M, HD), O_DT),
                   jax.ShapeDtypeStruct((M, LANES), F32)],
        scratch_shapes=[pltpu.VMEM((win, LANES), F32) for _ in range(n_stage)],
        compiler_params=pltpu.CompilerParams(
            dimension_semantics=("arbitrary", "arbitrary"), vmem_limit_bytes=_vmem_limit(est)),
        name=f"dilated_attn_g{gi}",
    )(qkv, qkv, qkv, qkv, qkv, qkv, qkv, bias)


def _combine_kernel(o0, o1, o2, l0, l1, l2, e_ref, out_ref):
    ls = [l0[...], l1[...], l2[...]]
    m = jnp.maximum(jnp.maximum(ls[0], ls[1]), ls[2])
    es = [jnp.exp(x - m) for x in ls]
    tot = es[0] + es[1] + es[2]
    acc = None
    sel = e_ref[...].astype(BF16)
    for o_ref, e in zip((o0, o1, o2), es):
        w = _segsum(e / tot, sel)
        term = w * o_ref[...].astype(F32)
        acc = term if acc is None else acc + term
    out_ref[...] = acc.astype(out_ref.dtype)


def _combine_groups(os_, lses, *, heads, tm=256):
    M, HD = os_[0].shape
    tm = min(tm, M)
    expand = np.zeros((LANES, HD), np.float32)
    for h in range(heads):
        expand[h, h * ATT_HEAD_DIM:(h + 1) * ATT_HEAD_DIM] = 1.0
    row = pl.BlockSpec((tm, HD), lambda i: (i, 0))
    lrow = pl.BlockSpec((tm, LANES), lambda i: (i, 0))
    est = 2 * 4 * _nbytes((tm, HD), BF16) + 6 * _nbytes((tm, HD), F32)
    return pl.pallas_call(
        _combine_kernel,
        grid=(M // tm,),
        in_specs=[row, row, row, lrow, lrow, lrow, pl.BlockSpec((LANES, HD), lambda i: (0, 0))],
        out_specs=row,
        out_shape=jax.ShapeDtypeStruct((M, HD), BF16),
        compiler_params=pltpu.CompilerParams(
            dimension_semantics=("arbitrary",), vmem_limit_bytes=_vmem_limit(est)),
        name="attn_combine",
    )(*os_, *lses, jnp.asarray(expand))


def _t5_bucket(rel):
    half = NUM_BUCKETS // 2
    max_exact = half // 2
    ret = jnp.where(rel > 0, half, 0)
    n = jnp.abs(rel)
    nf = jnp.maximum(n, 1).astype(F32)
    large = max_exact + (jnp.log(nf / max_exact) / math.log(MAX_DISTANCE / max_exact)
                         * (half - max_exact)).astype(jnp.int32)
    large = jnp.minimum(large, half - 1)
    return ret + jnp.where(n < max_exact, n, large)


def _rel_bias(table_g, dil, side):
    i = jnp.arange(side)[:, None]
    j = jnp.arange(3 * side)[None, :]
    rel = dil * (j - side - i)
    return jnp.transpose(table_g[_t5_bucket(rel)], (2, 0, 1)).astype(F32)


def _ep_cast(accs, rows, vecs):
    return tuple(accs)


def _ep_lora1(accs, rows, vecs):
    return jnp.tanh(accs[0]), accs[1], jax.nn.sigmoid(accs[2])


def _ep_lora2(accs, rows, vecs):
    lw = -math.exp(-0.5) * jax.nn.sigmoid(vecs[0] + accs[0])
    return lw, jax.nn.sigmoid(vecs[1] + accs[1])


def _ep_residual_norm(accs, rows, vecs):
    x_new = rows[0] + _rms(accs[0], vecs[0])
    return x_new, _rms(x_new, vecs[1])


def _ep_swiglu(accs, rows, vecs):
    return (jax.nn.silu(accs[0]) * accs[1],)


def _ep_ple(accs, rows, vecs):
    return (rows[0] + jax.nn.sigmoid(accs[0]) * accs[1],)


def _ep_ple_norm(accs, rows, vecs):
    x_new = rows[0] + jax.nn.sigmoid(accs[0]) * accs[1]
    return x_new, _rms(x_new, vecs[0])


def _pad_cols(w, n):
    return jnp.pad(w, ((0, 0), (0, n - w.shape[1])))


def _block_rows(w0, w1, n):
    h = n // 2
    z0 = jnp.zeros_like(w0)
    top = jnp.pad(jnp.concatenate([w0, z0], axis=1), ((0, h - w0.shape[0]), (0, 0)))
    bot = jnp.pad(jnp.concatenate([z0, w1], axis=1), ((0, h - w1.shape[0]), (0, 0)))
    return jnp.concatenate([top, bot], axis=0)


def _rwkv_layer(x, g_in, g_out, g_next, p, *, Mp, Lp, Ls):
    M, D = x.shape
    (mu, w_rkv, w0, w1, w2, a0, a1, a2, g1, g2, k_k, k_a, r_k, lnx_g, lnx_b, w_o) = p
    xs = _shift_mix(x, g_in[None], mu, Mp=Mp, Lp=Lp, Ls=Ls)
    wr, wk, wv = (w_rkv[i].astype(BF16) for i in range(3))
    r, = _fused_mm([(xs[0], wr, 0)], [], [], _ep_cast, [F32], D, tm=1024, tn=1024, name="proj_r")
    k, = _fused_mm([(xs[1], wk, 0)], [], [], _ep_cast, [F32], D, tm=1024, tn=1024, name="proj_k")
    v, = _fused_mm([(xs[2], wv, 0)], [], [], _ep_cast, [F32], D, tm=1024, tn=1024, name="proj_v")

    LP = 2 * LANES
    half = LP // 2
    w1c = jnp.concatenate([_pad_cols(w1[0], half), _pad_cols(w1[1], half)], axis=1).astype(BF16)
    a1c = jnp.concatenate([_pad_cols(a1[0], half), _pad_cols(a1[1], half)], axis=1).astype(BF16)
    g1c = _pad_cols(g1, LP).astype(BF16)
    tw, ta, sg = _fused_mm([(xs[3], w1c, 0), (xs[4], a1c, 0), (xs[5], g1c, 0)], [], [], _ep_lora1,
                           [BF16, BF16, BF16], LP, tm=512, tn=LP, name="lora_in")
    w2c = _block_rows(w2[0], w2[1], LP).astype(BF16)
    a2c = _block_rows(a2[0], a2[1], LP).astype(BF16)
    w0c = jnp.concatenate([w0[0], w0[1]])[None]
    a0c = jnp.concatenate([a0[0], a0[1]])[None]
    lw, a = _fused_mm([(tw, w2c, 0), (ta, a2c, 0)], [], [w0c, a0c], _ep_lora2, [F32, F32], 2 * D,
                      tm=512, tn=1024, name="lora_out_wa")
    g2c = jnp.pad(g2, ((0, LP - g2.shape[0]), (0, 0))).astype(BF16)
    g, = _fused_mm([(sg, g2c, 0)], [], [], _ep_cast, [F32], D, tm=512, tn=1024, name="lora_out_g")

    yf, yb = _wkv(r, k, v, lw, a, k_k[None], k_a[None], Mp=Mp, Lp=Lp, Ls=Ls)
    o = _wkv_post(yf, yb, r, k, v, a, g, k_a[None], r_k.reshape(1, D), lnx_g[None], lnx_b[None])
    return _fused_mm([(o, w_o.astype(BF16), 0)], [x], [g_out[None], g_next[None]], _ep_residual_norm,
                     [F32, BF16], D, tm=512, tn=D, name="rwkv_out")


def _attn_layer(x, xn, g_in, g_out, g_next, w_qkv, w_o, rel_table, *, Mp, Lp, Ls):
    M, D = x.shape
    heads = D // ATT_HEAD_DIM
    Wg = 3 * heads * ATT_HEAD_DIM
    w_bf = w_qkv.astype(BF16)
    os_, lses = [], []
    for gi, (win, dil) in enumerate(DILATED_GROUPS):
        side = win // (2 * dil)
        assert side == ATT_SIDE and Lp % (side * dil) == 0 and Ls % (side * dil) == 0
        xg = xn if dil == 1 else _norm_class_major(x, g_in[None], dil)
        qkv, = _fused_mm([(xg, w_bf, gi * Wg)], [], [], _ep_cast, [BF16], Wg, tm=1024, tn=1024,
                         name=f"qkv_g{gi}")
        bias = _rel_bias(rel_table[:, gi * heads:(gi + 1) * heads], dil, side)
        o, lse = _dilated_group(qkv, bias, gi, dil, Mp=Mp, Lp=Lp, Ls=Ls, heads=heads)
        os_.append(o)
        lses.append(lse)
    o = _combine_groups(os_, lses, heads=heads)
    return _fused_mm([(o, w_o.astype(BF16), 0)], [x], [g_out[None], g_next[None]], _ep_residual_norm,
                     [F32, BF16], D, tm=512, tn=D, name="attn_out")


def _ffn_ple(x, xn, g_out, ple_norm, w_gu, w_down, p, w_gate, w_proj, g_next):
    M, D = x.shape
    F = w_down.shape[0]
    wgu = w_gu.astype(BF16)
    hmid, = _fused_mm([(xn, wgu, 0), (xn, wgu, F)], [], [], _ep_swiglu, [BF16], F,
                      tm=1024, tn=512, name="ffn_up")
    x, xn = _fused_mm([(hmid, w_down.astype(BF16), 0)], [x], [g_out[None], ple_norm[None]],
                      _ep_residual_norm, [F32, BF16], D, tm=256, tn=D, name="ffn_down")
    dots = [(xn, w_gate.astype(BF16), 0), (p, w_proj.astype(BF16), 0)]
    if g_next is None:
        x, = _fused_mm(dots, [x], [], _ep_ple, [F32], D, tm=512, tn=D, name="ple_last")
        return x, None
    return _fused_mm(dots, [x], [g_next[None]], _ep_ple_norm, [F32, BF16], D, tm=512, tn=D, name="ple")


def _first_norm_kernel(x_ref, g_ref, o_ref):
    o_ref[...] = _rms(x_ref[...], g_ref[...]).astype(o_ref.dtype)


def _first_norm(x, g, *, tm=512):
    M, D = x.shape
    tm = min(tm, M)
    return pl.pallas_call(
        _first_norm_kernel,
        grid=(M // tm,),
        in_specs=[pl.BlockSpec((tm, D), lambda i: (i, 0)), pl.BlockSpec((1, D), lambda i: (0, 0))],
        out_specs=pl.BlockSpec((tm, D), lambda i: (i, 0)),
        out_shape=jax.ShapeDtypeStruct((M, D), BF16),
        compiler_params=pltpu.CompilerParams(dimension_semantics=("arbitrary",)),
        name="first_norm",
    )(x, g[None])


def kernel(x_prompt, x_sample, p_prompt, p_sample, norm_gains, rel_bias_table, rwkv_mu, rwkv_w_rkv, rwkv_w0, rwkv_w1, rwkv_w2, rwkv_a0, rwkv_a1, rwkv_a2, rwkv_g1, rwkv_g2, rwkv_k_k, rwkv_k_a, rwkv_r_k, rwkv_lnx_g, rwkv_lnx_b, rwkv_w_o, att_w_qkv, att_w_o, ffn_w_gu, ffn_w_down, ple_norm, ple_w_gate, ple_w_proj):
    Bp, Lp, D = x_prompt.shape
    Bs, Ls, _ = x_sample.shape
    Mp, Ms = Bp * Lp, Bs * Ls
    depth = norm_gains.shape[0]
    x = jnp.concatenate([x_prompt.reshape(Mp, D), x_sample.reshape(Ms, D)], axis=0)
    pdim = p_prompt.shape[-1]
    p = jnp.concatenate([p_prompt.reshape(depth, Mp, pdim), p_sample.reshape(depth, Ms, pdim)],
                        axis=1).astype(BF16)
    seq = dict(Mp=Mp, Lp=Lp, Ls=Ls)
    xn = None
    for i in range(depth):
        g = norm_gains[i]
        j = i // 2
        if i % 2 == 0:
            params = (rwkv_mu[j], rwkv_w_rkv[j], rwkv_w0[j], rwkv_w1[j], rwkv_w2[j], rwkv_a0[j], rwkv_a1[j],
                      rwkv_a2[j], rwkv_g1[j], rwkv_g2[j], rwkv_k_k[j], rwkv_k_a[j], rwkv_r_k[j],
                      rwkv_lnx_g[j], rwkv_lnx_b[j], rwkv_w_o[j])
            x, xn = _rwkv_layer(x, g[0], g[1], g[2], params, **seq)
        else:
            if xn is None:
                xn = _first_norm(x, g[0])
            x, xn = _attn_layer(x, xn, g[0], g[1], g[2], att_w_qkv[j], att_w_o[j], rel_bias_table, **seq)
        g_next = norm_gains[i + 1][0] if (i + 1 < depth and (i + 1) % 2 == 1) else None
        x, xn = _ffn_ple(x, xn, g[3], ple_norm[i], ffn_w_gu[i], ffn_w_down[i], p[i],
                         ple_w_gate[i], ple_w_proj[i], g_next)
    return x[:Mp].reshape(Bp, Lp, D), x[Mp:].reshape(Bs, Ls, D)
```

```python
import functools
import math

import numpy as np
import jax
import jax.numpy as jnp
from jax import lax
from jax.experimental import pallas as pl
from jax.experimental.pallas import tpu as pltpu

F32 = jnp.float32
BF16 = jnp.bfloat16

NORM_EPS = 1e-6
GN_EPS = 64e-5
RWKV_HEAD = 64
ATT_HEAD_DIM = 128
DILATED_GROUPS = ((128, 1), (512, 4), (2048, 16))
ATT_SCALE = 1.0 / math.sqrt(ATT_HEAD_DIM)
NEG_INF = -1e30
NUM_BUCKETS = 32
MAX_DISTANCE = 1024

LANES = 128
V7X_VMEM_BYTES = 64 * 1024 * 1024
VMEM_CAP = V7X_VMEM_BYTES - 8 * 1024 * 1024

WKV_CHUNK = 64
WKV_MERGE_BLOCKS = (1, 2, 4, 8, 16, 32)
ATT_SIDE = 64
O_DT = jnp.bfloat16


def _vmem_limit(est_bytes):
    return int(min(max(est_bytes + est_bytes // 2 + 8 * 1024 * 1024, 32 * 1024 * 1024), VMEM_CAP))


def _nbytes(shape, dtype):
    return int(np.prod(shape)) * jnp.dtype(dtype).itemsize


_DN = {"nn": (((1,), (0,)), ((), ())), "nt": (((1,), (1,)), ((), ())), "tn": (((0,), (0,)), ((), ()))}


def _dot(a, b, dims="nn", f32=False):
    if f32:
        return lax.dot_general(a.astype(F32), b.astype(F32), _DN[dims],
                               precision=lax.Precision.HIGHEST, preferred_element_type=F32)
    return lax.dot_general(a.astype(BF16), b.astype(BF16), _DN[dims], preferred_element_type=F32)


def _segsum(x, sel):
    hi = x.astype(BF16)
    lo = (x - hi.astype(F32)).astype(BF16)
    return _dot(hi, sel) + _dot(lo, sel)


def _rms(x, g):
    return x * lax.rsqrt(jnp.mean(x * x, axis=-1, keepdims=True) + NORM_EPS) * g


def _fused_mm_kernel(*refs, nd, nr, nv, no, nk, epilogue):
    d = refs[:2 * nd]
    r = refs[2 * nd:2 * nd + nr]
    v = refs[2 * nd + nr:2 * nd + nr + nv]
    o = refs[2 * nd + nr + nv:2 * nd + nr + nv + no]
    scratch = refs[2 * nd + nr + nv + no:]

    def finish(accs):
        outs = epilogue(accs, [x[...] for x in r], [x[...] for x in v])
        for o_ref, val in zip(o, outs):
            o_ref[...] = val.astype(o_ref.dtype)

    if nk == 1:
        finish([jnp.dot(d[2 * t][...], d[2 * t + 1][...], preferred_element_type=F32) for t in range(nd)])
    else:
        acc_ref = scratch[0]
        k = pl.program_id(2)

        @pl.when(k == 0)
        def _():
            acc_ref[...] = jnp.zeros_like(acc_ref)

        acc_ref[...] += jnp.dot(d[0][...], d[1][...], preferred_element_type=F32)

        @pl.when(k == nk - 1)
        def _():
            finish([acc_ref[...]])


def _fused_mm(dots, rows, vecs, epilogue, out_dtypes, n_cols, *, tm, tn, tk=None, name):
    M = dots[0][0].shape[0]
    tm = min(tm, M)
    tn = max(t for t in range(LANES, min(tn, n_cols) + 1, LANES)
             if n_cols % t == 0 and all(c0 % t == 0 for _, _, c0 in dots))
    assert M % tm == 0 and n_cols % tn == 0
    dots = [(x, w, c0 // tn) for x, w, c0 in dots]
    nd, nr, nv, no = len(dots), len(rows), len(vecs), len(out_dtypes)
    K0 = dots[0][0].shape[1]
    nk = 1
    if tk is not None and K0 > tk:
        assert nd == 1 and K0 % tk == 0
        nk = K0 // tk
    grid = (M // tm, n_cols // tn) + ((nk,) if nk > 1 else ())
    w_resident = nk == 1 and n_cols == tn
    in_specs = []
    est = 0
    for x, w, off in dots:
        K = x.shape[1]
        kb = tk if nk > 1 else K
        if nk > 1:
            in_specs.append(pl.BlockSpec((tm, kb), lambda i, j, k: (i, k)))
            in_specs.append(pl.BlockSpec((kb, tn), lambda i, j, k, off=off: (k, j + off)))
        elif w_resident:
            in_specs.append(pl.BlockSpec((tm, kb), lambda i, j: (i, 0)))
            in_specs.append(pl.BlockSpec((kb, tn), lambda i, j, off=off: (0, j + off),
                                         pipeline_mode=pl.Buffered(1)))
        else:
            in_specs.append(pl.BlockSpec((tm, kb), lambda i, j: (i, 0)))
            in_specs.append(pl.BlockSpec((kb, tn), lambda i, j, off=off: (0, j + off)))
        est += 2 * _nbytes((tm, kb), x.dtype) + (1 if w_resident else 2) * _nbytes((kb, tn), w.dtype) \
            + 2 * _nbytes((tm, tn), F32)
    for a in rows:
        in_specs.append(pl.BlockSpec((tm, tn), lambda i, j, *k: (i, j)))
        est += 2 * _nbytes((tm, tn), a.dtype)
    for a in vecs:
        in_specs.append(pl.BlockSpec((1, tn), lambda i, j, *k: (0, j)))
    out_specs = [pl.BlockSpec((tm, tn), lambda i, j, *k: (i, j)) for _ in out_dtypes]
    est += sum(2 * _nbytes((tm, tn), dt) for dt in out_dtypes)
    scratch = [pltpu.VMEM((tm, tn), F32)] if nk > 1 else []
    args = []
    for x, w, _ in dots:
        args += [x, w]
    args += list(rows) + list(vecs)
    outs = pl.pallas_call(
        functools.partial(_fused_mm_kernel, nd=nd, nr=nr, nv=nv, no=no, nk=nk, epilogue=epilogue),
        grid=grid,
        in_specs=in_specs,
        out_specs=out_specs,
        out_shape=[jax.ShapeDtypeStruct((M, n_cols), dt) for dt in out_dtypes],
        scratch_shapes=scratch,
        compiler_params=pltpu.CompilerParams(
            dimension_semantics=("arbitrary",) * len(grid), vmem_limit_bytes=_vmem_limit(est)),
        name=name,
    )(*args)
    return outs


def _seq_pos(r0, Mp, Lp, Ls):
    in_prompt = r0 < Mp
    L = jnp.where(in_prompt, Lp, Ls)
    pos = lax.rem(r0 - jnp.where(in_prompt, 0, Mp), L)
    return pos, L


def _shift_mix_kernel(x_ref, xp_ref, xn_ref, g_ref, mu_ref, *outs, tm, Mp, Lp, Ls):
    r0 = pl.program_id(0) * tm
    pos, L = _seq_pos(r0, Mp, Lp, Ls)
    g = g_ref[...]
    h = _rms(x_ref[...], g)
    hp = _rms(xp_ref[...], g)[7:8]
    hn = _rms(xn_ref[...], g)[0:1]
    hp = jnp.where(pos == 0, 0.0, hp)
    hn = jnp.where(pos + tm == L, 0.0, hn)
    row = lax.broadcasted_iota(jnp.int32, (tm, 1), 0)
    prev = jnp.where(row == 0, hp, pltpu.roll(h, 1, 0))
    nxt = jnp.where(row == tm - 1, hn, pltpu.roll(h, tm - 1, 0))
    xx = 0.5 * (prev + nxt) - h
    for j, o_ref in enumerate(outs):
        o_ref[...] = (h + xx * mu_ref[j:j + 1, :]).astype(o_ref.dtype)


def _shift_mix(x, g, mu, *, Mp, Lp, Ls, tm=256):
    M, D = x.shape
    tm = min(tm, Lp, Ls)
    nb8 = M // 8
    est = 2 * _nbytes((tm, D), F32) + 12 * _nbytes((tm, D), BF16) + 6 * _nbytes((tm, D), F32)
    return pl.pallas_call(
        functools.partial(_shift_mix_kernel, tm=tm, Mp=Mp, Lp=Lp, Ls=Ls),
        grid=(M // tm,),
        in_specs=[
            pl.BlockSpec((tm, D), lambda i: (i, 0)),
            pl.BlockSpec((8, D), lambda i: (jnp.maximum(i * (tm // 8) - 1, 0), 0)),
            pl.BlockSpec((8, D), lambda i: (jnp.minimum((i + 1) * (tm // 8), nb8 - 1), 0)),
            pl.BlockSpec((1, D), lambda i: (0, 0)),
            pl.BlockSpec((6, D), lambda i: (0, 0)),
        ],
        out_specs=[pl.BlockSpec((tm, D), lambda i: (i, 0)) for _ in range(6)],
        out_shape=[jax.ShapeDtypeStruct((M, D), BF16) for _ in range(6)],
        compiler_params=pltpu.CompilerParams(
            dimension_semantics=("arbitrary",), vmem_limit_bytes=_vmem_limit(est)),
        name="shift_mix",
    )(x, x, x, g, mu)


def _wkv_masks():
    C = WKV_CHUNK
    P = 2 * C
    t = np.arange(P)[:, None]
    s = np.arange(P)[None, :]
    bd = (t // C) == (s // C)
    tw = np.arange(C)[:, None]
    sw = np.arange(P)[None, :] % C
    sq, wide, tri = [], [], []
    for rev in (False, True):
        ms = [bd]
        for b in WKV_MERGE_BLOCKS:
            same = (t // (2 * b)) == (s // (2 * b))
            hi_t, hi_s = (t // b) % 2, (s // b) % 2
            ms.append(same & (hi_t == (0 if rev else 1)) & (hi_s == (1 if rev else 0)))
        sq.append(np.stack(ms))
        strict = (sw < tw) if not rev else (sw > tw)
        incl = (sw <= tw) if not rev else (sw >= tw)
        wide.append(np.stack([strict, incl]))
        tc, sc = np.arange(C)[:, None], np.arange(C)[None, :]
        tri.append((sc <= tc) if not rev else (sc >= tc))
    return (np.stack(sq).astype(np.float32), np.stack(wide).astype(np.float32),
            np.stack(tri).astype(np.float32))


def _split3(x):
    hi = x.astype(BF16)
    r1 = x - hi.astype(F32)
    mid = r1.astype(BF16)
    lo = (r1 - mid.astype(F32)).astype(BF16)
    return hi, mid, lo


def _wkv_kernel(rf, kf, vf, lwf, af, rb_, kb_, vb_, lwb, ab, kk_ref, ka_ref, sqm_ref, wdm_ref, tri_ref,
                yf_ref, yb_ref, s_ref, *, npairs, nc, Mp, Lp, Ls):
    c = pl.program_id(1)
    C = WKV_CHUNK
    pos_f, _ = _seq_pos(c * C, Mp, Lp, Ls)
    pos_b, L_b = _seq_pos((nc - 1 - c) * C, Mp, Lp, Ls)

    @pl.when(pos_f == 0)
    def _():
        s_ref[0] = jnp.zeros_like(s_ref[0])

    @pl.when(pos_b + C == L_b)
    def _():
        s_ref[1] = jnp.zeros_like(s_ref[1])

    ones_bd = sqm_ref[0, 0].astype(BF16)
    bd = sqm_ref[0, 0] > 0
    eye = (lax.broadcasted_iota(jnp.int32, (2 * C, 2 * C), 0)
           == lax.broadcasted_iota(jnp.int32, (2 * C, 2 * C), 1)).astype(F32)
    dirs = ((rf, kf, vf, lwf, af, yf_ref), (rb_, kb_, vb_, lwb, ab, yb_ref))
    units = [(d, p) for d in range(2) for p in range(npairs)]
    sls = [slice(p * LANES, (p + 1) * LANES) for p in range(npairs)]

    def stack_bd(x):
        return jnp.where(bd, jnp.concatenate([x, x], axis=0), 0.0)

    G_dir = []
    for d in range(2):
        lw = dirs[d][3][...]
        hi, mid, lo = _split3(lw)
        tri = tri_ref[d].astype(BF16)
        G_dir.append(_dot(tri, hi) + (_dot(tri, mid) + _dot(tri, lo)))

    kr = [dirs[d][1][:, sls[p]] * kk_ref[:, sls[p]] for d, p in units]
    ss = _segsum(jnp.concatenate([x * x for x in kr], axis=0), ones_bd)

    st = []
    for i, (d, p) in enumerate(units):
        r_ref, k_ref, v_ref, lw_ref, a_ref, _ = dirs[d]
        sl = sls[p]
        kk = kr[i] * lax.rsqrt(ss[i * C:(i + 1) * C] + 1e-12)
        lw, a, k = lw_ref[:, sl], a_ref[:, sl], k_ref[:, sl]
        G = G_dir[d][:, sl]
        g_end = jnp.sum(lw, axis=0, keepdims=True)
        to_end = jnp.exp(g_end - G)
        inv_gam = jnp.exp(-G)
        kj = k * (1.0 + (a - 1.0) * ka_ref[:, sl])
        beta = kk * a
        u = dict(v_st=stack_bd(v_ref[:, sl]), v=v_ref[:, sl], decay=jnp.exp(g_end),
                 bk=jnp.concatenate([beta * to_end, kj * to_end], axis=0))
        u["lhs"] = jnp.concatenate([-kk * jnp.exp(G - lw), r_ref[:, sl] * jnp.exp(G)], axis=0)
        u["rhs"] = jnp.concatenate([stack_bd(beta * inv_gam), stack_bd(kj * inv_gam)], axis=0)
        st.append(u)

    for u in st:
        u["x1"] = _dot(u["lhs"], u.pop("rhs"), "nt")
    for u, (d, p) in zip(st, units):
        strict, incl = wdm_ref[d, 0] > 0, wdm_ref[d, 1] > 0
        x1 = u.pop("x1")
        a_ab = jnp.where(strict, x1[:C, :LANES], 0.0)
        u["a_ak"] = jnp.where(strict, x1[:C, LANES:], 0.0)
        u["m_rb"] = jnp.where(incl, x1[C:, :LANES], 0.0)
        u["m_rk"] = jnp.where(incl, x1[C:, LANES:], 0.0)
        u["aa"] = jnp.concatenate([a_ab, a_ab], axis=0)
    for u, (d, p) in zip(st, units):
        u["tinv"] = eye + u["aa"] * sqm_ref[d, 1]
    for lvl in range(2, len(WKV_MERGE_BLOCKS) + 1):
        for u, (d, p) in zip(st, units):
            u["m"] = _dot(u["aa"] * sqm_ref[d, lvl], u["tinv"])
        for u in st:
            u["tinv"] = u["tinv"] + _dot(u["tinv"], u.pop("m"))
    for u in st:
        xyv = _dot(jnp.concatenate([u["a_ak"], u["m_rk"]], axis=0), u["v_st"])
        u["xv"], u["yv"] = xyv[:C], xyv[C:]
        tinv = u.pop("tinv")
        u["tw"] = tinv[:C] + tinv[C:]

    S = [s_ref[d, p] for d, p in units]
    xy = [_dot(u["lhs"], s, "nt") for u, s in zip(st, S)]
    xs = [z[:C] + u["xv"] for u, z in zip(st, xy)]
    ys = [z[C:] + u["yv"] for u, z in zip(st, xy)]
    us = [_dot(u["tw"], stack_bd(x)) for u, x in zip(st, xs)]
    for i, (d, p) in enumerate(units):
        u = st[i]
        dirs[d][5][:, sls[p]] = ys[i] + _dot(u["m_rb"], stack_bd(us[i]))
        uv = jnp.concatenate([us[i], u["v"]], axis=0)
        s_ref[d, p] = jnp.where(bd, S[i] * u["decay"] + _dot(uv, u["bk"], "tn"), 0.0)


def _wkv(r, k, v, lw, a, k_k, k_a, *, Mp, Lp, Ls, lane_block=1024):
    M, D = r.shape
    C = WKV_CHUNK
    lb = min(lane_block, D)
    nlb, nc = D // lb, M // C
    sqm, wdm, tri = (jnp.asarray(m) for m in _wkv_masks())
    fwd = lambda j, c: (c, j)
    bwd = lambda j, c: (nc - 1 - c, j)
    bwd2 = lambda j, c: (nc - 1 - c, nlb + j)
    blk = lambda im: pl.BlockSpec((C, lb), im)
    const = lambda shape: pl.BlockSpec(shape, lambda j, c: (0,) * len(shape))
    est = 24 * _nbytes((C, lb), F32) + 4 * _nbytes(sqm.shape, F32)
    return pl.pallas_call(
        functools.partial(_wkv_kernel, npairs=lb // LANES, nc=nc, Mp=Mp, Lp=Lp, Ls=Ls),
        grid=(nlb, nc),
        in_specs=[blk(fwd), blk(fwd), blk(fwd), blk(fwd), blk(fwd),
                  blk(bwd), blk(bwd), blk(bwd), blk(bwd2), blk(bwd2),
                  pl.BlockSpec((1, lb), lambda j, c: (0, j)), pl.BlockSpec((1, lb), lambda j, c: (0, j)),
                  const(sqm.shape), const(wdm.shape), const(tri.shape)],
        out_specs=[blk(fwd), blk(bwd)],
        out_shape=[jax.ShapeDtypeStruct((M, D), F32)] * 2,
        scratch_shapes=[pltpu.VMEM((2, lb // LANES, LANES, LANES), F32)],
        compiler_params=pltpu.CompilerParams(
            dimension_semantics=("arbitrary", "arbitrary"), vmem_limit_bytes=_vmem_limit(est)),
        name="wkv7_bidir",
    )(r, k, v, lw, a, r, k, v, lw, a, k_k, k_a, sqm, wdm, tri)


def _wkv_post_kernel(yf_ref, yb_ref, r_ref, k_ref, v_ref, a0_ref, a1_ref, g_ref,
                     ka_ref, rk_ref, lg_ref, lb_ref, ones_ref, o_ref):
    ones_bd = ones_ref[...].astype(BF16)
    D = o_ref.shape[1]
    inv_n = 1.0 / RWKV_HEAD
    for p in range(D // LANES):
        sl = slice(p * LANES, (p + 1) * LANES)
        y = yf_ref[:, sl] + yb_ref[:, sl]
        mean = _segsum(y, ones_bd) * inv_n
        dlt = y - mean
        var = _segsum(dlt * dlt, ones_bd) * inv_n
        yn = dlt * lax.rsqrt(var + GN_EPS) * lg_ref[:, sl] + lb_ref[:, sl]
        k, ka = k_ref[:, sl], ka_ref[:, sl]
        k0 = k * (1.0 + (a0_ref[:, sl] - 1.0) * ka)
        k1 = k * (1.0 + (a1_ref[:, sl] - 1.0) * ka)
        kb = 0.5 * k0 + 0.5 * k1
        bonus = _segsum(r_ref[:, sl] * kb * rk_ref[:, sl], ones_bd) * v_ref[:, sl]
        o_ref[:, sl] = ((yn + bonus) * g_ref[:, sl]).astype(o_ref.dtype)


def _wkv_post(yf, yb, r, k, v, a, g, k_a, r_k, lnx_g, lnx_b, *, tm=256):
    M, D = r.shape
    tm = min(tm, M)
    row = pl.BlockSpec((tm, D), lambda i: (i, 0))
    row2 = pl.BlockSpec((tm, D), lambda i: (i, 1))
    vec = pl.BlockSpec((1, D), lambda i: (0, 0))
    P = 2 * RWKV_HEAD
    ones_bd = jnp.asarray((np.arange(P)[:, None] // RWKV_HEAD == np.arange(P)[None, :] // RWKV_HEAD)
                          .astype(np.float32))
    est = 2 * 8 * _nbytes((tm, D), F32) + 2 * _nbytes((tm, D), BF16)
    return pl.pallas_call(
        _wkv_post_kernel,
        grid=(M // tm,),
        in_specs=[row, row, row, row, row, row, row2, row, vec, vec, vec, vec,
                  pl.BlockSpec((P, P), lambda i: (0, 0))],
        out_specs=row,
        out_shape=jax.ShapeDtypeStruct((M, D), BF16),
        compiler_params=pltpu.CompilerParams(
            dimension_semantics=("arbitrary",), vmem_limit_bytes=_vmem_limit(est)),
        name="wkv_post",
    )(yf, yb, r, k, v, a, a, g, k_a, r_k, lnx_g, lnx_b, ones_bd)


N_STAGE = 4


def _norm_permute_kernel(x_ref, g_ref, o_ref, *stage, dil):
    side = ATT_SIDE
    x = x_ref[...]
    inv = lax.rsqrt(jnp.mean(x * x, axis=-1, keepdims=True) + NORM_EPS)
    for j in range(x_ref.shape[1] // LANES):
        sl = slice(j * LANES, (j + 1) * LANES)
        buf = stage[j % len(stage)]
        buf[...] = x_ref[:, sl] * inv * g_ref[:, sl]
        for c in range(dil):
            o_ref[c * side:(c + 1) * side, sl] = buf[pl.ds(c, side, stride=dil), :].astype(o_ref.dtype)


def _norm_class_major(x, g, dil):
    M, D = x.shape
    win = ATT_SIDE * dil
    spec = pl.BlockSpec((win, D), lambda i: (i, 0))
    est = 3 * _nbytes((win, D), F32) + 2 * _nbytes((win, D), BF16) + N_STAGE * _nbytes((win, LANES), F32)
    return pl.pallas_call(
        functools.partial(_norm_permute_kernel, dil=dil),
        grid=(M // win,),
        in_specs=[spec, pl.BlockSpec((1, D), lambda i: (0, 0))],
        out_specs=spec,
        out_shape=jax.ShapeDtypeStruct((M, D), BF16),
        scratch_shapes=[pltpu.VMEM((win, LANES), F32) for _ in range(N_STAGE)],
        compiler_params=pltpu.CompilerParams(
            dimension_semantics=("arbitrary",), vmem_limit_bytes=_vmem_limit(est)),
        name=f"norm_class_major_d{dil}",
    )(x, g)


def _attn_kernel(q_ref, kp_ref, kc_ref, kn_ref, vp_ref, vc_ref, vn_ref, bias_ref, o_ref, lse_ref, *stage,
                 heads, dil, nw_prompt_total, nw_p, nw_s):
    n = pl.program_id(0)
    c = pl.program_id(1)
    side = ATT_SIDE
    in_prompt = n < nw_prompt_total
    nw = jnp.where(in_prompt, nw_p, nw_s)
    pos = lax.rem(n - jnp.where(in_prompt, 0, nw_prompt_total), nw)
    has_prev = pos > 0
    has_next = pos < nw - 1
    i = lax.broadcasted_iota(jnp.int32, (side, 3 * side), 0)
    j = lax.broadcasted_iota(jnp.int32, (side, 3 * side), 1)
    off = j - side - i
    valid = (jnp.abs(off) <= side) & ((j >= side) | has_prev) & ((j < 2 * side) | has_next)
    lane = lax.broadcasted_iota(jnp.int32, (side, LANES), 1)
    hs = [slice(h * ATT_HEAD_DIM, (h + 1) * ATT_HEAD_DIM) for h in range(heads)]
    logits = [_dot(q_ref[:, sl], jnp.concatenate([kp_ref[:, sl], kc_ref[:, sl], kn_ref[:, sl]], axis=0), "nt")
              for sl in hs]
    probs = []
    lse_all = jnp.zeros((side, LANES), F32)
    for h in range(heads):
        s = jnp.where(valid, logits[h] * ATT_SCALE + bias_ref[h], NEG_INF)
        m = jnp.max(s, axis=-1, keepdims=True)
        p = jnp.exp(s - m)
        l = jnp.sum(p, axis=-1, keepdims=True)
        probs.append((p / l).astype(BF16))
        lse_all = jnp.where(lane == h, m + jnp.log(l), lse_all)
    outs = []
    for h, sl in enumerate(hs):
        vcat = jnp.concatenate([vp_ref[:, sl], vc_ref[:, sl], vn_ref[:, sl]], axis=0)
        outs.append(_dot(probs[h], vcat))
    if dil == 1:
        for h, sl in enumerate(hs):
            o_ref[:, sl] = outs[h].astype(o_ref.dtype)
        lse_ref[...] = lse_all
    else:
        rows = pl.ds(c, side, stride=dil)
        for h in range(heads):
            stage[h][rows, :] = outs[h]
        stage[heads][rows, :] = lse_all

        @pl.when(c == dil - 1)
        def _():
            for h, sl in enumerate(hs):
                o_ref[:, sl] = stage[h][...].astype(o_ref.dtype)
            lse_ref[...] = stage[heads][...]


def _dilated_group(qkv, bias, gi, dil, *, Mp, Lp, Ls, heads):
    M, W = qkv.shape
    HD = heads * ATT_HEAD_DIM
    side = ATT_SIDE
    win = side * dil
    nwt = M // win
    cur = lambda t: pl.BlockSpec((side, HD), lambda n, c, t=t: (n * dil + c, t))
    prv = lambda t: pl.BlockSpec((side, HD), lambda n, c, t=t: (jnp.maximum(n - 1, 0) * dil + c, t))
    nxt = lambda t: pl.BlockSpec((side, HD), lambda n, c, t=t: (jnp.minimum(n + 1, nwt - 1) * dil + c, t))
    n_stage = heads + 1 if dil > 1 else 0
    est = 2 * 7 * _nbytes((side, HD), BF16) + 2 * _nbytes(bias.shape, F32) + 2 * _nbytes((win, HD), O_DT) \
        + 2 * _nbytes((win, LANES), F32) + 3 * heads * _nbytes((side, 3 * side), F32) \
        + n_stage * _nbytes((win, LANES), F32)
    return pl.pallas_call(
        functools.partial(_attn_kernel, heads=heads, dil=dil, nw_prompt_total=Mp // win,
                          nw_p=Lp // win, nw_s=Ls // win),
        grid=(nwt, dil),
        in_specs=[cur(0), prv(1), cur(1), nxt(1), prv(2), cur(2), nxt(2),
                  pl.BlockSpec(bias.shape, lambda n, c: (0, 0, 0))],
        out_specs=[pl.BlockSpec((win, HD), lambda n, c: (n, 0)),
                   pl.BlockSpec((win, LANES), lambda n, c: (n, 0))],
        out_shape=[jax.ShapeDtypeStruct((M, HD), O_DT),
                   jax.ShapeDtypeStruct((M, LANES), F32)],
        scratch_shapes=[pltpu.VMEM((win, LANES), F32) for _ in range(n_stage)],
        compiler_params=pltpu.CompilerParams(
            dimension_semantics=("arbitrary", "arbitrary"), vmem_limit_bytes=_vmem_limit(est)),
        name=f"dilated_attn_g{gi}",
    )(qkv, qkv, qkv, qkv, qkv, qkv, qkv, bias)


def _combine_kernel(o0, o1, o2, l0, l1, l2, e_ref, out_ref):
    ls = [l0[...], l1[...], l2[...]]
    m = jnp.maximum(jnp.maximum(ls[0], ls[1]), ls[2])
    es = [jnp.exp(x - m) for x in ls]
    tot = es[0] + es[1] + es[2]
    acc = None
    sel = e_ref[...].astype(BF16)
    for o_ref, e in zip((o0, o1, o2), es):
        w = _segsum(e / tot, sel)
        term = w * o_ref[...].astype(F32)
        acc = term if acc is None else acc + term
    out_ref[...] = acc.astype(out_ref.dtype)


def _combine_groups(os_, lses, *, heads, tm=256):
    M, HD = os_[0].shape
    tm = min(tm, M)
    expand = np.zeros((LANES, HD), np.float32)
    for h in range(heads):
        expand[h, h * ATT_HEAD_DIM:(h + 1) * ATT_HEAD_DIM] = 1.0
    row = pl.BlockSpec((tm, HD), lambda i: (i, 0))
    lrow = pl.BlockSpec((tm, LANES), lambda i: (i, 0))
    est = 2 * 4 * _nbytes((tm, HD), BF16) + 6 * _nbytes((tm, HD), F32)
    return pl.pallas_call(
        _combine_kernel,
        grid=(M // tm,),
        in_specs=[row, row, row, lrow, lrow, lrow, pl.BlockSpec((LANES, HD), lambda i: (0, 0))],
        out_specs=row,
        out_shape=jax.ShapeDtypeStruct((M, HD), BF16),
        compiler_params=pltpu.CompilerParams(
            dimension_semantics=("arbitrary",), vmem_limit_bytes=_vmem_limit(est)),
        name="attn_combine",
    )(*os_, *lses, jnp.asarray(expand))


def _t5_bucket(rel):
    half = NUM_BUCKETS // 2
    max_exact = half // 2
    ret = jnp.where(rel > 0, half, 0)
    n = jnp.abs(rel)
    nf = jnp.maximum(n, 1).astype(F32)
    large = max_exact + (jnp.log(nf / max_exact) / math.log(MAX_DISTANCE / max_exact)
                         * (half - max_exact)).astype(jnp.int32)
    large = jnp.minimum(large, half - 1)
    return ret + jnp.where(n < max_exact, n, large)


def _rel_bias(table_g, dil, side):
    i = jnp.arange(side)[:, None]
    j = jnp.arange(3 * side)[None, :]
    rel = dil * (j - side - i)
    return jnp.transpose(table_g[_t5_bucket(rel)], (2, 0, 1)).astype(F32)


def _ep_cast(accs, rows, vecs):
    return tuple(accs)


def _ep_lora1(accs, rows, vecs):
    return jnp.tanh(accs[0]), accs[1], jax.nn.sigmoid(accs[2])


def _ep_lora2(accs, rows, vecs):
    lw = -math.exp(-0.5) * jax.nn.sigmoid(vecs[0] + accs[0])
    return lw, jax.nn.sigmoid(vecs[1] + accs[1])


def _ep_residual_norm(accs, rows, vecs):
    x_new = rows[0] + _rms(accs[0], vecs[0])
    return x_new, _rms(x_new, vecs[1])


def _ep_swiglu(accs, rows, vecs):
    return (jax.nn.silu(accs[0]) * accs[1],)


def _ep_ple(accs, rows, vecs):
    return (rows[0] + jax.nn.sigmoid(accs[0]) * accs[1],)


def _ep_ple_norm(accs, rows, vecs):
    x_new = rows[0] + jax.nn.sigmoid(accs[0]) * accs[1]
    return x_new, _rms(x_new, vecs[0])


def _pad_cols(w, n):
    return jnp.pad(w, ((0, 0), (0, n - w.shape[1])))


def _block_rows(w0, w1, n):
    h = n // 2
    z0 = jnp.zeros_like(w0)
    top = jnp.pad(jnp.concatenate([w0, z0], axis=1), ((0, h - w0.shape[0]), (0, 0)))
    bot = jnp.pad(jnp.concatenate([z0, w1], axis=1), ((0, h - w1.shape[0]), (0, 0)))
    return jnp.concatenate([top, bot], axis=0)


def _rwkv_layer(x, g_in, g_out, g_next, p, *, Mp, Lp, Ls):
    M, D = x.shape
    (mu, w_rkv, w0, w1, w2, a0, a1, a2, g1, g2, k_k, k_a, r_k, lnx_g, lnx_b, w_o) = p
    xs = _shift_mix(x, g_in[None], mu, Mp=Mp, Lp=Lp, Ls=Ls)
    wr, wk, wv = (w_rkv[i].astype(BF16) for i in range(3))
    r, = _fused_mm([(xs[0], wr, 0)], [], [], _ep_cast, [F32], D, tm=1024, tn=1024, name="proj_r")
    k, = _fused_mm([(xs[1], wk, 0)], [], [], _ep_cast, [F32], D, tm=1024, tn=1024, name="proj_k")
    v, = _fused_mm([(xs[2], wv, 0)], [], [], _ep_cast, [F32], D, tm=1024, tn=1024, name="proj_v")

    LP = 2 * LANES
    half = LP // 2
    w1c = jnp.concatenate([_pad_cols(w1[0], half), _pad_cols(w1[1], half)], axis=1).astype(BF16)
    a1c = jnp.concatenate([_pad_cols(a1[0], half), _pad_cols(a1[1], half)], axis=1).astype(BF16)
    g1c = _pad_cols(g1, LP).astype(BF16)
    tw, ta, sg = _fused_mm([(xs[3], w1c, 0), (xs[4], a1c, 0), (xs[5], g1c, 0)], [], [], _ep_lora1,
                           [BF16, BF16, BF16], LP, tm=512, tn=LP, name="lora_in")
    w2c = _block_rows(w2[0], w2[1], LP).astype(BF16)
    a2c = _block_rows(a2[0], a2[1], LP).astype(BF16)
    w0c = jnp.concatenate([w0[0], w0[1]])[None]
    a0c = jnp.concatenate([a0[0], a0[1]])[None]
    lw, a = _fused_mm([(tw, w2c, 0), (ta, a2c, 0)], [], [w0c, a0c], _ep_lora2, [F32, F32], 2 * D,
                      tm=512, tn=1024, name="lora_out_wa")
    g2c = jnp.pad(g2, ((0, LP - g2.shape[0]), (0, 0))).astype(BF16)
    g, = _fused_mm([(sg, g2c, 0)], [], [], _ep_cast, [F32], D, tm=512, tn=1024, name="lora_out_g")

    yf, yb = _wkv(r, k, v, lw, a, k_k[None], k_a[None], Mp=Mp, Lp=Lp, Ls=Ls)
    o = _wkv_post(yf, yb, r, k, v, a, g, k_a[None], r_k.reshape(1, D), lnx_g[None], lnx_b[None])
    return _fused_mm([(o, w_o.astype(BF16), 0)], [x], [g_out[None], g_next[None]], _ep_residual_norm,
                     [F32, BF16], D, tm=512, tn=D, name="rwkv_out")


def _attn_layer(x, xn, g_in, g_out, g_next, w_qkv, w_o, rel_table, *, Mp, Lp, Ls):
    M, D = x.shape
    heads = D // ATT_HEAD_DIM
    Wg = 3 * heads * ATT_HEAD_DIM
    w_bf = w_qkv.astype(BF16)
    os_, lses = [], []
    for gi, (win, dil) in enumerate(DILATED_GROUPS):
        side = win // (2 * dil)
        assert side == ATT_SIDE and Lp % (side * dil) == 0 and Ls % (side * dil) == 0
        xg = xn if dil == 1 else _norm_class_major(x, g_in[None], dil)
        qkv, = _fused_mm([(xg, w_bf, gi * Wg)], [], [], _ep_cast, [BF16], Wg, tm=1024, tn=1024,
                         name=f"qkv_g{gi}")
        bias = _rel_bias(rel_table[:, gi * heads:(gi + 1) * heads], dil, side)
        o, lse = _dilated_group(qkv, bias, gi, dil, Mp=Mp, Lp=Lp, Ls=Ls, heads=heads)
        os_.append(o)
        lses.append(lse)
    o = _combine_groups(os_, lses, heads=heads)
    return _fused_mm([(o, w_o.astype(BF16), 0)], [x], [g_out[None], g_next[None]], _ep_residual_norm,
                     [F32, BF16], D, tm=512, tn=D, name="attn_out")


def _ffn_ple(x, xn, g_out, ple_norm, w_gu, w_down, p, w_gate, w_proj, g_next):
    M, D = x.shape
    F = w_down.shape[0]
    wgu = w_gu.astype(BF16)
    hmid, = _fused_mm([(xn, wgu, 0), (xn, wgu, F)], [], [], _ep_swiglu, [BF16], F,
                      tm=1024, tn=512, name="ffn_up")
    x, xn = _fused_mm([(hmid, w_down.astype(BF16), 0)], [x], [g_out[None], ple_norm[None]],
                      _ep_residual_norm, [F32, BF16], D, tm=256, tn=D, name="ffn_down")
    dots = [(xn, w_gate.astype(BF16), 0), (p, w_proj.astype(BF16), 0)]
    if g_next is None:
        x, = _fused_mm(dots, [x], [], _ep_ple, [F32], D, tm=512, tn=D, name="ple_last")
        return x, None
    return _fused_mm(dots, [x], [g_next[None]], _ep_ple_norm, [F32, BF16], D, tm=512, tn=D, name="ple")


def _first_norm_kernel(x_ref, g_ref, o_ref):
    o_ref[...] = _rms(x_ref[...], g_ref[...]).astype(o_ref.dtype)


def _first_norm(x, g, *, tm=512):
    M, D = x.shape
    tm = min(tm, M)
    return pl.pallas_call(
        _first_norm_kernel,
        grid=(M // tm,),
        in_specs=[pl.BlockSpec((tm, D), lambda i: (i, 0)), pl.BlockSpec((1, D), lambda i: (0, 0))],
        out_specs=pl.BlockSpec((tm, D), lambda i: (i, 0)),
        out_shape=jax.ShapeDtypeStruct((M, D), BF16),
        compiler_params=pltpu.CompilerParams(dimension_semantics=("arbitrary",)),
        name="first_norm",
    )(x, g[None])


def kernel(x_prompt, x_sample, p_prompt, p_sample, norm_gains, rel_bias_table, rwkv_mu, rwkv_w_rkv, rwkv_w0, rwkv_w1, rwkv_w2, rwkv_a0, rwkv_a1, rwkv_a2, rwkv_g1, rwkv_g2, rwkv_k_k, rwkv_k_a, rwkv_r_k, rwkv_lnx_g, rwkv_lnx_b, rwkv_w_o, att_w_qkv, att_w_o, ffn_w_gu, ffn_w_down, ple_norm, ple_w_gate, ple_w_proj):
    Bp, Lp, D = x_prompt.shape
    Bs, Ls, _ = x_sample.shape
    Mp, Ms = Bp * Lp, Bs * Ls
    depth = norm_gains.shape[0]
    x = jnp.concatenate([x_prompt.reshape(Mp, D), x_sample.reshape(Ms, D)], axis=0)
    pdim = p_prompt.shape[-1]
    p = jnp.concatenate([p_prompt.reshape(depth, Mp, pdim), p_sample.reshape(depth, Ms, pdim)],
                        axis=1).astype(BF16)
    seq = dict(Mp=Mp, Lp=Lp, Ls=Ls)
    xn = None
    for i in range(depth):
        g = norm_gains[i]
        j = i // 2
        if i % 2 == 0:
            params = (rwkv_mu[j], rwkv_w_rkv[j], rwkv_w0[j], rwkv_w1[j], rwkv_w2[j], rwkv_a0[j], rwkv_a1[j],
                      rwkv_a2[j], rwkv_g1[j], rwkv_g2[j], rwkv_k_k[j], rwkv_k_a[j], rwkv_r_k[j],
                      rwkv_lnx_g[j], rwkv_lnx_b[j], rwkv_w_o[j])
            x, xn = _rwkv_layer(x, g[0], g[1], g[2], params, **seq)
        else:
            if xn is None:
                xn = _first_norm(x, g[0])
            x, xn = _attn_layer(x, xn, g[0], g[1], g[2], att_w_qkv[j], att_w_o[j], rel_bias_table, **seq)
        g_next = norm_gains[i + 1][0] if (i + 1 < depth and (i + 1) % 2 == 1) else None
        x, xn = _ffn_ple(x, xn, g[3], ple_norm[i], ffn_w_gu[i], ffn_w_down[i], p[i],
                         ple_w_gate[i], ple_w_proj[i], g_next)
    return x[:Mp].reshape(Bp, Lp, D), x[Mp:].reshape(Bs, Ls, D)
```
